```python
import jax
import jax.numpy as jnp
from jax import lax
import numpy as np

D_MODEL = 4096
BATCH = 4
SEQ = 2048
DEPTH = 4
DEC_BATCH = 8
DEC_SEQ = 4
PAST_LEN = 8192
PAGE_SIZE = 128

N_MIXERS = 3
N_MLSTM = (DEPTH + 2) // 3
N_MOBA = (DEPTH + 1) // 3
N_POOL = DEPTH // 3

ML_HEADS = 8
ML_DQK = D_MODEL // ML_HEADS // 2
ML_DV = D_MODEL // ML_HEADS
ML_CHUNK = 128
ML_M_INIT = -1.0e30
ML_PROJ = 2 * ML_HEADS * ML_DQK + 2 * ML_HEADS * ML_DV + 2 * ML_HEADS

MOBA_HEADS = 32
MOBA_HD = D_MODEL // MOBA_HEADS
MOBA_BLOCK = 256
MOBA_TOPK = 3
MOBA_QCHUNK = 8
ROT_DIM = MOBA_HD // 4
ROPE_THETA = 500000.0

POOL_WINDOWS = (2, 4, 8, 16)
POOL_GROUP = D_MODEL // len(POOL_WINDOWS)
POOL_HIST = max(POOL_WINDOWS) - 1

N_EXPERTS = 32
TOP_K = 4
EXPERT_FF = D_MODEL // 4
SWIGLU_LIMIT = 7.0
SWIGLU_ALPHA = 1.702
MOE_BLOCK = 128

DN_ALPHA = (2.0 * DEPTH) ** 0.25
DN_BETA = (8.0 * DEPTH) ** -0.25
LN_EPS = 1e-5
RMS_EPS = 1e-6

kernel_name = 'hybrid_mlstm_moba_pool_moe_decode_step'


def layer_norm(x, g, b):
    xf = x.astype(jnp.float32)
    mu = jnp.mean(xf, -1, keepdims=True)
    var = jnp.mean(jnp.square(xf - mu), -1, keepdims=True)
    return ((xf - mu) * lax.rsqrt(var + LN_EPS) * g + b).astype(x.dtype)


def mlstm_chunk(carry, q, k, v, logi, logf):
    c_mat, n_vec, m_prev = carry
    L = q.shape[2]
    b = jnp.cumsum(logf, axis=-1)
    causal = jnp.tril(jnp.ones((L, L), dtype=bool))
    logw = jnp.where(causal, b[..., :, None] - b[..., None, :] + logi[..., None, :], -jnp.inf)
    m_inter = m_prev[..., None] + b
    m_t = jnp.maximum(m_inter, jnp.max(logw, axis=-1))
    s = jnp.einsum('bhtd,bhsd->bhts', q, k) * jnp.exp(logw - m_t[..., None])
    inter = jnp.exp(m_inter - m_t)
    num = jnp.einsum('bhts,bhsv->bhtv', s, v) + inter[..., None] * jnp.einsum('bhtd,bhdv->bhtv', q, c_mat)
    den = jnp.sum(s, -1) + inter * jnp.einsum('bhtd,bhd->bht', q, n_vec)
    h = num / jnp.maximum(jnp.abs(den), jnp.exp(-m_t))[..., None]
    m_new = m_t[..., -1]
    w_end = jnp.exp(b[..., -1:] - b + logi - m_new[..., None])
    decay = jnp.exp(m_prev + b[..., -1] - m_new)
    c_new = decay[..., None, None] * c_mat + jnp.einsum('bhs,bhsd,bhsv->bhdv', w_end, k, v)
    n_new = decay[..., None] * n_vec + jnp.einsum('bhs,bhsd->bhd', w_end, k)
    return (c_new, n_new, m_new), h


def mlstm_mixer(u, w_in, b_gates, norm_g, w_out, carry, chunk):
    B, S, _ = u.shape
    H, DK, DV = ML_HEADS, ML_DQK, ML_DV
    proj = u @ w_in
    q, k, v, o, gates = jnp.split(proj, [H * DK, 2 * H * DK, 2 * H * DK + H * DV, 2 * H * DK + 2 * H * DV], axis=-1)

    def heads(t, d):
        return t.reshape(B, S, H, d).transpose(0, 2, 1, 3).astype(jnp.float32)

    q = heads(q, DK) * (DK ** -0.5)
    k = heads(k, DK)
    v = heads(v, DV)
    gates = (gates + b_gates).astype(jnp.float32).transpose(0, 2, 1)
    logi = gates[:, :H]
    logf = jax.nn.log_sigmoid(gates[:, H:])
    nc = S // chunk

    def to_chunks(t):
        return jnp.moveaxis(t.reshape(t.shape[:2] + (nc, chunk) + t.shape[3:]), 2, 0)

    carry, h = lax.scan(lambda cr, xs_: mlstm_chunk(cr, *xs_), carry,
                        (to_chunks(q), to_chunks(k), to_chunks(v), to_chunks(logi), to_chunks(logf)))
    h = jnp.moveaxis(h, 0, 2).reshape(B, H, S, DV)
    h = h * lax.rsqrt(jnp.mean(h * h, -1, keepdims=True) + RMS_EPS)
    h = h.transpose(0, 2, 1, 3).reshape(B, S, H * DV) * norm_g
    y = (jax.nn.sigmoid(o.astype(jnp.float32)) * h).astype(u.dtype) @ w_out
    return y, carry


def partial_rope(x, pos):
    half = ROT_DIM // 2
    inv = ROPE_THETA ** (-jnp.arange(half, dtype=jnp.float32) * 2.0 / ROT_DIM)
    ang = pos.astype(jnp.float32)[:, None] * inv[None, :]
    cos = jnp.cos(ang)[None, :, None, :]
    sin = jnp.sin(ang)[None, :, None, :]
    xr = x[..., :ROT_DIM].astype(jnp.float32)
    x1, x2 = xr[..., :half], xr[..., half:]
    rot = jnp.concatenate([x1 * cos - x2 * sin, x2 * cos + x1 * sin], -1)
    return jnp.concatenate([rot.astype(x.dtype), x[..., ROT_DIM:]], -1)


def moba_attention(q, k, v, q_pos):
    B, Q, H, hd = q.shape
    T = k.shape[1]
    NB = -(-T // MOBA_BLOCK)
    pad = NB * MOBA_BLOCK - T

    def blocks(t):
        t = jnp.pad(t, ((0, 0), (0, pad), (0, 0), (0, 0)))
        return t.reshape(B, NB, MOBA_BLOCK, H, hd).transpose(0, 3, 1, 2, 4)

    kb, vb = blocks(k), blocks(v)
    kmean = jnp.mean(kb.astype(jnp.float32), axis=3)
    n_sel = min(MOBA_TOPK, NB - 1)
    qc = MOBA_QCHUNK if Q % MOBA_QCHUNK == 0 else Q
    nq = Q // qc
    q_chunks = q.reshape(B, nq, qc, H, hd).transpose(1, 0, 3, 2, 4)
    pos_chunks = q_pos.reshape(nq, qc)
    bi = jnp.arange(B)[:, None, None, None]
    hi = jnp.arange(H)[None, :, None, None]

    def chunk(args):
        qh, pos = args
        own = pos // MOBA_BLOCK
        own_b = jnp.broadcast_to(own[None, None, :, None], (B, H, qc, 1))
        if n_sel > 0:
            gate = jnp.einsum('bhqd,bhnd->bhqn', qh.astype(jnp.float32), kmean)
            fully_past = jnp.arange(NB)[None, :] < own[:, None]
            gate = jnp.where(fully_past[None, None], gate, -jnp.inf)
            top_v, top_i = lax.top_k(gate, n_sel)
            idx = jnp.concatenate([top_i, own_b], -1)
            ok = jnp.concatenate([jnp.isfinite(top_v), jnp.ones((B, H, qc, 1), bool)], -1)
        else:
            idx = own_b
            ok = jnp.ones((B, H, qc, 1), bool)
        gk = kb[bi, hi, idx]
        gv = vb[bi, hi, idx]
        s = jnp.einsum('bhqd,bhqnkd->bhqnk', qh, gk).astype(jnp.float32) * (hd ** -0.5)
        kpos = idx[..., None] * MOBA_BLOCK + jnp.arange(MOBA_BLOCK)
        mask = ok[..., None] & (kpos <= pos[None, None, :, None, None])
        s = jnp.where(mask, s, -jnp.inf)
        p = jax.nn.softmax(s.reshape(B, H, qc, -1), axis=-1).reshape(s.shape)
        return jnp.einsum('bhqnk,bhqnkd->bhqd', p.astype(gv.dtype), gv)

    outs = lax.map(chunk, (q_chunks, pos_chunks))
    return outs.transpose(1, 0, 3, 2, 4).reshape(B, Q, H, hd)


def moba_mixer(u, w_qkv, w_out, k_past, v_past):
    B, S, _ = u.shape
    P = 0 if k_past is None else k_past.shape[1]
    qkv = (u @ w_qkv).reshape(B, S, 3, MOBA_HEADS, MOBA_HD)
    pos = P + jnp.arange(S)
    q = partial_rope(qkv[:, :, 0], pos)
    k = partial_rope(qkv[:, :, 1], pos)
    v = qkv[:, :, 2]
    k_all = k if k_past is None else jnp.concatenate([k_past.astype(k.dtype), k], 1)
    v_all = v if v_past is None else jnp.concatenate([v_past.astype(v.dtype), v], 1)
    o = moba_attention(q, k_all, v_all, pos)
    return o.reshape(B, S, D_MODEL) @ w_out, (k, v)


def pool_mixer(u, hist, pool_w, pool_scale):
    B, S, D = u.shape
    ext = u if hist is None else jnp.concatenate([hist.astype(u.dtype), u], 1)
    T = ext.shape[1]
    n_prev = T - S
    cs = jnp.concatenate([jnp.zeros((B, 1, D), jnp.float32), jnp.cumsum(ext.astype(jnp.float32), axis=1)], 1)
    idx = jnp.arange(n_prev, T)
    upper = cs[:, idx + 1]
    groups = []
    for g, w in enumerate(POOL_WINDOWS):
        sl = slice(g * POOL_GROUP, (g + 1) * POOL_GROUP)
        lo = jnp.maximum(idx + 1 - w, 0)
        cnt = jnp.minimum(idx + 1, w).astype(jnp.float32)
        mean = (upper[..., sl] - cs[:, lo, sl]) / cnt[None, :, None]
        groups.append(mean - u[..., sl].astype(jnp.float32))
    d = jnp.stack(groups, axis=2).astype(u.dtype)
    y = jnp.einsum('bsgi,gio->bsgo', d, pool_w).reshape(B, S, D) * pool_scale
    return y, ext[:, T - POOL_HIST:]


def moe_ffn(u, router_w, router_b, w_gu, b_gu, w_down, b_down):
    B, S, D = u.shape
    xt = u.reshape(B * S, D)
    T = B * S
    logits = (xt @ router_w + router_b).astype(jnp.float32)
    top_v, top_i = lax.top_k(logits, TOP_K)
    gates = jax.nn.softmax(top_v, axis=-1)
    A = T * TOP_K
    flat_e = top_i.reshape(-1)
    order = jnp.argsort(flat_e)
    sorted_e = flat_e[order]
    counts = jnp.bincount(flat_e, length=N_EXPERTS)
    padded = (counts + MOE_BLOCK - 1) // MOE_BLOCK * MOE_BLOCK
    pad_end = jnp.cumsum(padded)
    pad_start = pad_end - padded
    start = jnp.cumsum(counts) - counts
    dest = pad_start[sorted_e] + jnp.arange(A) - start[sorted_e]
    n_blocks = -(-A // MOE_BLOCK) + N_EXPERTS
    P = n_blocks * MOE_BLOCK
    slot_tok = jnp.full((P,), T, jnp.int32).at[dest].set((order // TOP_K).astype(jnp.int32))
    slot_gate = jnp.zeros((P,), jnp.float32).at[dest].set(gates.reshape(-1)[order])
    blk_exp = jnp.minimum(jnp.searchsorted(pad_end, jnp.arange(n_blocks) * MOE_BLOCK, side='right'), N_EXPERTS - 1)
    x_pad = jnp.concatenate([xt, jnp.zeros((1, D), xt.dtype)], 0)
    xs_blk = x_pad[slot_tok].reshape(n_blocks, MOE_BLOCK, D)

    def expert(args):
        xb, e = args
        gu = xb @ w_gu[e] + b_gu[e]
        g, up = gu[:, :EXPERT_FF], gu[:, EXPERT_FF:]
        g = jnp.minimum(g, SWIGLU_LIMIT)
        up = jnp.clip(up, -SWIGLU_LIMIT, SWIGLU_LIMIT)
        h = g * jax.nn.sigmoid(SWIGLU_ALPHA * g) * (up + 1.0)
        return h @ w_down[e] + b_down[e]

    ys = lax.map(expert, (xs_blk, blk_exp)).reshape(P, D)
    y = jax.ops.segment_sum(ys * slot_gate[:, None].astype(ys.dtype), slot_tok, num_segments=T + 1)[:T]
    return y.reshape(B, S, D)


def apply_layer(x, c, mixer, ada_w_i, ada_b_i, ln_g_i, ln_b_i, moe_params):
    mod = (jax.nn.silu(c) @ ada_w_i + ada_b_i)[:, None, :]
    sh_a, sc_a, g_a, sh_f, sc_f, g_f = jnp.split(mod, 6, axis=-1)
    y, st = mixer(x * (1.0 + sc_a) + sh_a)
    x = layer_norm(DN_ALPHA * x + g_a * y, ln_g_i[0], ln_b_i[0])
    f = moe_ffn(x * (1.0 + sc_f) + sh_f, *moe_params)
    x = layer_norm(DN_ALPHA * x + g_f * f, ln_g_i[1], ln_b_i[1])
    return x, st


def setup_inputs(seed: int = 0) -> dict:
    key = jax.random.key(seed)
    ks = iter(jax.random.split(key, 40))

    def nrm(shape, scale=1.0):
        return jax.random.normal(next(ks), shape, jnp.float32) * scale

    n_pages = PAST_LEN // PAGE_SIZE
    n_pool_pages = (DEC_BATCH * n_pages * 5) // 4
    perm = jax.random.permutation(next(ks), n_pool_pages)
    page_table = perm[:DEC_BATCH * n_pages].reshape(DEC_BATCH, n_pages).astype(jnp.int32)
    D = D_MODEL
    return {
        'x_prompt': nrm((BATCH, SEQ, D)),
        'x_sample': nrm((DEC_BATCH, DEC_SEQ, D)),
        'c_prompt': nrm((BATCH, D)),
        'c_sample': nrm((DEC_BATCH, D)),
        'state_mlstm_C': nrm((N_MLSTM, DEC_BATCH, ML_HEADS, ML_DQK, ML_DV)),
        'state_mlstm_n': nrm((N_MLSTM, DEC_BATCH, ML_HEADS, ML_DQK)),
        'state_mlstm_m': nrm((N_MLSTM, DEC_BATCH, ML_HEADS)),
        'cache_moba_k': nrm((N_MOBA, n_pool_pages, PAGE_SIZE, MOBA_HEADS, MOBA_HD)),
        'cache_moba_v': nrm((N_MOBA, n_pool_pages, PAGE_SIZE, MOBA_HEADS, MOBA_HD)),
        'page_table': page_table,
        'state_pool': nrm((N_POOL, DEC_BATCH, POOL_HIST, D)),
        'ada_w': nrm((DEPTH, D, 6 * D), D ** -0.5),
        'ada_b': nrm((DEPTH, 6 * D), 0.02),
        'ln_g': 1.0 + nrm((DEPTH, 2, D), 0.02),
        'ln_b': nrm((DEPTH, 2, D), 0.02),
        'mlstm_w_in': nrm((N_MLSTM, D, ML_PROJ), D ** -0.5),
        'mlstm_b_gates': jnp.concatenate([nrm((N_MLSTM, ML_HEADS), 0.1),
                                          3.0 + nrm((N_MLSTM, ML_HEADS), 0.5)], -1),
        'mlstm_norm_g': 1.0 + nrm((N_MLSTM, D), 0.02),
        'mlstm_w_out': nrm((N_MLSTM, ML_HEADS * ML_DV, D), (ML_HEADS * ML_DV) ** -0.5 * DN_BETA),
        'moba_w_qkv': nrm((N_MOBA, D, 3 * D), D ** -0.5),
        'moba_w_out': nrm((N_MOBA, D, D), D ** -0.5 * DN_BETA),
        'pool_w': nrm((N_POOL, len(POOL_WINDOWS), POOL_GROUP, POOL_GROUP), POOL_GROUP ** -0.5 * DN_BETA),
        'pool_scale': 1.0 + nrm((N_POOL, D), 0.02),
        'router_w': nrm((DEPTH, D, N_EXPERTS), D ** -0.5),
        'router_b': nrm((DEPTH, N_EXPERTS), 0.01),
        'moe_w_gu': nrm((DEPTH, N_EXPERTS, D, 2 * EXPERT_FF), D ** -0.5),
        'moe_b_gu': nrm((DEPTH, N_EXPERTS, 2 * EXPERT_FF), 0.02),
        'moe_w_down': nrm((DEPTH, N_EXPERTS, EXPERT_FF, D), EXPERT_FF ** -0.5 * DN_BETA),
        'moe_b_down': nrm((DEPTH, N_EXPERTS, D), 0.02),
    }


def reference(x_prompt, x_sample, c_prompt, c_sample, state_mlstm_C, state_mlstm_n, state_mlstm_m,
              cache_moba_k, cache_moba_v, page_table, state_pool,
              ada_w, ada_b, ln_g, ln_b, mlstm_w_in, mlstm_b_gates, mlstm_norm_g, mlstm_w_out,
              moba_w_qkv, moba_w_out, pool_w, pool_scale,
              router_w, router_b, moe_w_gu, moe_b_gu, moe_w_down, moe_b_down):
    xp, xs = x_prompt, x_sample
    bp = x_prompt.shape[0]
    sp = x_prompt.shape[1]
    ss = x_sample.shape[1]
    n_dec, n_pages = page_table.shape
    ml_p, ml_s, mb_p, mb_s, pl_p, pl_s = [], [], [], [], [], []
    for i in range(DEPTH):
        kind, j = i % N_MIXERS, i // N_MIXERS
        moe_params = (router_w[i], router_b[i], moe_w_gu[i], moe_b_gu[i], moe_w_down[i], moe_b_down[i])
        lp = (ada_w[i], ada_b[i], ln_g[i], ln_b[i], moe_params)
        if kind == 0:
            w = (mlstm_w_in[j], mlstm_b_gates[j], mlstm_norm_g[j], mlstm_w_out[j])
            carry_p = (jnp.zeros((bp, ML_HEADS, ML_DQK, ML_DV), jnp.float32),
                       jnp.zeros((bp, ML_HEADS, ML_DQK), jnp.float32),
                       jnp.full((bp, ML_HEADS), ML_M_INIT, jnp.float32))
            carry_s = (state_mlstm_C[j].astype(jnp.float32), state_mlstm_n[j].astype(jnp.float32),
                       state_mlstm_m[j].astype(jnp.float32))
            xp, st_p = apply_layer(xp, c_prompt, lambda u: mlstm_mixer(u, *w, carry_p, min(ML_CHUNK, sp)), *lp)
            xs, st_s = apply_layer(xs, c_sample, lambda u: mlstm_mixer(u, *w, carry_s, ss), *lp)
            ml_p.append(st_p)
            ml_s.append(st_s)
        elif kind == 1:
            k_past = cache_moba_k[j][page_table].reshape(n_dec, n_pages * PAGE_SIZE, MOBA_HEADS, MOBA_HD)
            v_past = cache_moba_v[j][page_table].reshape(n_dec, n_pages * PAGE_SIZE, MOBA_HEADS, MOBA_HD)
            xp, st_p = apply_layer(xp, c_prompt, lambda u: moba_mixer(u, moba_w_qkv[j], moba_w_out[j], None, None), *lp)
            xs, st_s = apply_layer(xs, c_sample, lambda u: moba_mixer(u, moba_w_qkv[j], moba_w_out[j], k_past, v_past), *lp)
            mb_p.append(st_p)
            mb_s.append(st_s)
        else:
            xp, st_p = apply_layer(xp, c_prompt, lambda u: pool_mixer(u, None, pool_w[j], pool_scale[j]), *lp)
            xs, st_s = apply_layer(xs, c_sample, lambda u: pool_mixer(u, state_pool[j], pool_w[j], pool_scale[j]), *lp)
            pl_p.append(st_p)
            pl_s.append(st_s)
    sdt = state_mlstm_C.dtype
    mlstm_C_prompt = jnp.stack([s[0] for s in ml_p]).astype(x_prompt.dtype)
    mlstm_n_prompt = jnp.stack([s[1] for s in ml_p]).astype(x_prompt.dtype)
    mlstm_m_prompt = jnp.stack([s[2] for s in ml_p]).astype(x_prompt.dtype)
    mlstm_C_sample = jnp.stack([s[0] for s in ml_s]).astype(sdt)
    mlstm_n_sample = jnp.stack([s[1] for s in ml_s]).astype(sdt)
    mlstm_m_sample = jnp.stack([s[2] for s in ml_s]).astype(sdt)
    moba_k_prompt = jnp.stack([s[0] for s in mb_p])
    moba_v_prompt = jnp.stack([s[1] for s in mb_p])
    moba_k_sample = jnp.stack([s[0] for s in mb_s])
    moba_v_sample = jnp.stack([s[1] for s in mb_s])
    pool_prompt = jnp.stack(pl_p)
    pool_sample = jnp.stack(pl_s)
    return (xp, xs, mlstm_C_prompt, mlstm_n_prompt, mlstm_m_prompt,
            mlstm_C_sample, mlstm_n_sample, mlstm_m_sample,
            moba_k_prompt, moba_v_prompt, moba_k_sample, moba_v_sample,
            pool_prompt, pool_sample)
```

```python
import functools

import jax
import jax.numpy as jnp
from jax import lax
from jax.experimental import pallas as pl
from jax.experimental.pallas import tpu as pltpu

F32 = jnp.float32
BF16 = jnp.bfloat16
HIGHEST = lax.Precision.HIGHEST

N_MIXERS = 3
ML_M_INIT = -1.0e30
ML_CHUNK = 128
MOBA_BLOCK = 256
MOBA_TOPK = 3
ROPE_THETA = 500000.0
POOL_WINDOWS = (2, 4, 8, 16)
TOP_K = 4
SWIGLU_LIMIT = 7.0
SWIGLU_ALPHA = 1.702
LN_EPS = 1e-5
RMS_EPS = 1e-6

LANES = 128
VMEM_LIMIT = 56 * 1024 * 1024
MM_TM = 1024
MM_TN = 512
LN_TS = 128
MOE_TM = 256
MOE_TN = 512
MOE_TN2 = 2048
ML_L = 256
ML_L_MIN = 16
POOL_TS = 256
NEG = -1.0e30


def _cp(*sem):
    return pltpu.CompilerParams(dimension_semantics=sem, vmem_limit_bytes=VMEM_LIMIT)


def _ada_kernel(c_ref, w_ref, b_ref, o_ref):
    c = c_ref[...]
    a = (c * jax.nn.sigmoid(c)).astype(BF16)
    o_ref[...] = jnp.dot(a, w_ref[...].astype(BF16), preferred_element_type=F32) + b_ref[...]


def ada_mod(c_all, ada_w, ada_b):
    depth, d, n = ada_w.shape
    r = c_all.shape[0]
    tn = min(MM_TN, n)
    return pl.pallas_call(
        _ada_kernel,
        grid=(depth, n // tn),
        in_specs=[pl.BlockSpec((r, d), lambda l, j: (0, 0)),
                  pl.BlockSpec((None, d, tn), lambda l, j: (l, 0, j)),
                  pl.BlockSpec((None, 1, tn), lambda l, j: (l, 0, j))],
        out_specs=pl.BlockSpec((None, r, tn), lambda l, j: (l, 0, j)),
        out_shape=jax.ShapeDtypeStruct((depth, r, n), F32),
        compiler_params=_cp("arbitrary", "arbitrary"),
        name="ada_mod",
    )(c_all, ada_w, ada_b.reshape(depth, 1, n))


def _ln_mod_kernel(*refs, res, has_ln, has_mod, has_aux, alpha):
    it = iter(refs)
    x_ref = next(it)
    x = x_ref[...]
    if has_ln:
        if res == "dense":
            y = next(it)[...].astype(F32)
        else:
            ys_ref = next(it)
            gt = next(it)[...]
            y = gt[:, 0:1] * ys_ref[0].astype(F32)
            for k in range(1, ys_ref.shape[0]):
                y = y + gt[:, k:k + 1] * ys_ref[k].astype(F32)
        gate = next(it)[...]
        lg = next(it)[...]
        lb = next(it)[...]
        z = alpha * x + gate * y
        mu = jnp.mean(z, axis=-1, keepdims=True)
        zc = z - mu
        var = jnp.mean(zc * zc, axis=-1, keepdims=True)
        x = zc * lax.rsqrt(var + LN_EPS) * lg + lb
    if has_mod:
        sc = next(it)[...]
        sh = next(it)[...]
        u = x * (1.0 + sc) + sh
    if has_aux:
        wa = next(it)[...]
        ba = next(it)[...]
    if has_ln:
        next(it)[...] = x
    if has_mod:
        u_ref = next(it)
        u_ref[...] = u.astype(u_ref.dtype)
    if has_aux:
        next(it)[...] = jnp.dot(u, wa, precision=HIGHEST, preferred_element_type=F32) + ba


def ln_mod(x, mods, *, y=None, ys=None, gates=None, gate_c=None, ln_g=None, ln_b=None,
           mod_c=None, aux_w=None, aux_b=None, u_dtype=BF16, alpha=1.0):
    b, s, d = x.shape
    ts = min(LN_TS, s)
    grid = (b, s // ts)
    row = pl.BlockSpec((None, ts, d), lambda i, j: (i, j, 0))

    def modspec(c):
        return pl.BlockSpec((None, 1, d), lambda i, j, c=c: (i, 0, c))

    vec = pl.BlockSpec((1, d), lambda i, j: (0, 0))
    has_ln = gate_c is not None
    has_mod = mod_c is not None
    has_aux = aux_w is not None
    res = "dense" if y is not None else "moe"
    args, specs = [x], [row]
    if has_ln:
        if y is not None:
            args.append(y)
            specs.append(row)
        else:
            k = ys.shape[0]
            args += [ys, gates]
            specs += [pl.BlockSpec((k, None, ts, d), lambda i, j: (0, i, j, 0)),
                      pl.BlockSpec((None, ts, k), lambda i, j: (i, j, 0))]
        args += [mods, ln_g.reshape(1, d), ln_b.reshape(1, d)]
        specs += [modspec(gate_c), vec, vec]
    if has_mod:
        args += [mods, mods]
        specs += [modspec(mod_c[1]), modspec(mod_c[0])]
    if has_aux:
        na = aux_w.shape[1]
        args += [aux_w, aux_b.reshape(1, na)]
        specs += [pl.BlockSpec((d, na), lambda i, j: (0, 0)), pl.BlockSpec((1, na), lambda i, j: (0, 0))]
    out_shape, out_specs = [], []
    if has_ln:
        out_shape.append(jax.ShapeDtypeStruct((b, s, d), F32))
        out_specs.append(row)
    if has_mod:
        out_shape.append(jax.ShapeDtypeStruct((b, s, d), u_dtype))
        out_specs.append(row)
    if has_aux:
        out_shape.append(jax.ShapeDtypeStruct((b, s, na), F32))
        out_specs.append(pl.BlockSpec((None, ts, na), lambda i, j: (i, j, 0)))
    return pl.pallas_call(
        functools.partial(_ln_mod_kernel, res=res, has_ln=has_ln, has_mod=has_mod, has_aux=has_aux, alpha=alpha),
        grid=grid, in_specs=specs, out_specs=out_specs, out_shape=out_shape,
        compiler_params=_cp("parallel", "parallel"),
        name="ln_mod",
    )(*args)


def _mm_kernel(x_ref, w_ref, o_ref, wb_ref):
    @pl.when(pl.program_id(1) == 0)
    def _():
        wb_ref[...] = w_ref[...].astype(BF16)

    o_ref[...] = jnp.dot(x_ref[...], wb_ref[...], preferred_element_type=F32).astype(o_ref.dtype)


def matmul(x, w, layer, col0, n, out_dtype):
    m, k = x.shape
    tm = min(MM_TM, m)
    tn = min(MM_TN, n)
    assert m % tm == 0 and n % tn == 0 and col0 % tn == 0
    c0 = col0 // tn
    return pl.pallas_call(
        _mm_kernel,
        grid=(n // tn, m // tm),
        in_specs=[pl.BlockSpec((tm, k), lambda j, i: (i, 0)),
                  pl.BlockSpec((None, k, tn), lambda j, i: (layer, 0, c0 + j))],
        out_specs=pl.BlockSpec((tm, tn), lambda j, i: (i, j)),
        out_shape=jax.ShapeDtypeStruct((m, n), out_dtype),
        scratch_shapes=[pltpu.VMEM((k, tn), BF16)],
        compiler_params=_cp("arbitrary", "arbitrary"),
        name="matmul",
    )(x, w)


def _split3(x):
    hi = x.astype(BF16)
    r1 = x - hi.astype(F32)
    mid = r1.astype(BF16)
    lo = (r1 - mid.astype(F32)).astype(BF16)
    return hi, mid, lo


def _log_sigmoid(x):
    return jnp.minimum(x, 0.0) - jnp.log(1.0 + jnp.exp(-jnp.abs(x)))


def _mlstm_kernel(q_ref, k_ref, v_ref, o_ref, gr_ref, gc_ref, ng_ref, c0_ref, n0_ref, m0_ref,
                  h_ref, co_ref, no_ref, mo_ref, c_sc, n_sc, m_sc, *, scale):
    c_idx = pl.program_id(2)
    nc = pl.num_programs(2)
    L = q_ref.shape[0]

    @pl.when(c_idx == 0)
    def _():
        c_sc[...] = c0_ref[...]
        n_sc[...] = n0_ref[...]
        m_sc[...] = m0_ref[...]

    q = q_ref[...]
    k = k_ref[...]
    v = v_ref[...]
    gr = gr_ref[...]
    gc = gc_ref[...]
    logi_r = gr[0:1, :]
    logf_r = _log_sigmoid(gr[1:2, :])
    logi_c = gc[:, 0:1]
    logf_c = _log_sigmoid(gc[:, 1:2])
    ri = lax.broadcasted_iota(jnp.int32, (L, L), 0)
    ci = lax.broadcasted_iota(jnp.int32, (L, L), 1)
    causal = ci <= ri
    upper = jnp.where(ri <= ci, 1.0, 0.0).astype(BF16)
    lower = jnp.where(ci <= ri, 1.0, 0.0).astype(BF16)
    fr = jnp.broadcast_to(logf_r, (8, L))
    fc = jnp.broadcast_to(logf_c, (L, LANES))
    b_r = sum(jnp.dot(p, upper, preferred_element_type=F32) for p in _split3(fr))[0:1, :]
    b_c = sum(jnp.dot(lower, p, preferred_element_type=F32) for p in _split3(fc))[:, 0:1]
    m_prev = m_sc[...]
    logw = jnp.where(causal, b_c - b_r + logi_r, -jnp.inf)
    m_inter = m_prev + b_c
    m_t = jnp.maximum(m_inter, jnp.max(logw, axis=-1, keepdims=True))
    s = lax.dot_general(q, k, (((1,), (1,)), ((), ())), preferred_element_type=F32)
    s = s * scale * jnp.exp(logw - m_t)
    inter = jnp.exp(m_inter - m_t)
    c_mat = c_sc[...]
    n_vec = n_sc[...]
    qf = q.astype(F32) * scale
    num = jnp.dot(s.astype(BF16), v, preferred_element_type=F32) + inter * (
        jnp.dot(q, c_mat.astype(BF16), preferred_element_type=F32) * scale)
    den = jnp.sum(s, axis=-1, keepdims=True) + inter * jnp.sum(qf * n_vec, axis=-1, keepdims=True)
    h = num / jnp.maximum(jnp.abs(den), jnp.exp(-m_t))
    m_new = m_t[L - 1:L, :]
    b_last = b_c[L - 1:L, :]
    w_end = jnp.exp(b_last - b_c + logi_c - m_new)
    decay = jnp.exp(m_prev + b_last - m_new)
    kw = k.astype(F32) * w_end
    c_new = decay * c_mat + lax.dot_general(kw.astype(BF16), v, (((0,), (0,)), ((), ())),
                                            preferred_element_type=F32)
    n_new = decay * n_vec + jnp.sum(kw, axis=0, keepdims=True)
    c_sc[...] = c_new
    n_sc[...] = n_new
    m_sc[...] = m_new
    hn = h * lax.rsqrt(jnp.mean(h * h, axis=-1, keepdims=True) + RMS_EPS) * ng_ref[...]
    h_ref[...] = (jax.nn.sigmoid(o_ref[...].astype(F32)) * hn).astype(h_ref.dtype)

    @pl.when(c_idx == nc - 1)
    def _():
        co_ref[...] = c_new
        no_ref[...] = n_new
        mo_ref[...] = m_new


def mlstm_core(proj, gates, norm_g, c0, n0, m0, bsz, seq, L):
    h_, dk, dv = c0.shape[1:]
    nc = seq // L
    g = gates.reshape(bsz, nc, L, 2, h_)
    g_row = g.transpose(0, 4, 1, 3, 2)
    g_col = g.transpose(0, 4, 1, 2, 3)
    kq, kk, kv, ko = 0, h_ * dk // dk, 2 * h_ * dk // dv, (2 * h_ * dk + h_ * dv) // dv
    row = lambda b, h, c: b * nc + c
    out = pl.pallas_call(
        functools.partial(_mlstm_kernel, scale=float(dk) ** -0.5),
        grid=(bsz, h_, nc),
        in_specs=[pl.BlockSpec((L, dk), lambda b, h, c: (row(b, h, c), kq + h)),
                  pl.BlockSpec((L, dk), lambda b, h, c: (row(b, h, c), kk + h)),
                  pl.BlockSpec((L, dv), lambda b, h, c: (row(b, h, c), kv + h)),
                  pl.BlockSpec((L, dv), lambda b, h, c: (row(b, h, c), ko + h)),
                  pl.BlockSpec((None, None, None, 2, L), lambda b, h, c: (b, h, c, 0, 0)),
                  pl.BlockSpec((None, None, None, L, 2), lambda b, h, c: (b, h, c, 0, 0)),
                  pl.BlockSpec((1, dv), lambda b, h, c: (0, h)),
                  pl.BlockSpec((None, None, dk, dv), lambda b, h, c: (b, h, 0, 0)),
                  pl.BlockSpec((None, None, 1, dk), lambda b, h, c: (b, h, 0, 0)),
                  pl.BlockSpec((None, None, 1, 1), lambda b, h, c: (b, h, 0, 0))],
        out_specs=[pl.BlockSpec((L, dv), lambda b, h, c: (row(b, h, c), h)),
                   pl.BlockSpec((None, None, dk, dv), lambda b, h, c: (b, h, 0, 0)),
                   pl.BlockSpec((None, None, 1, dk), lambda b, h, c: (b, h, 0, 0)),
                   pl.BlockSpec((None, None, 1, 1), lambda b, h, c: (b, h, 0, 0))],
        out_shape=[jax.ShapeDtypeStruct((bsz * seq, h_ * dv), BF16),
                   jax.ShapeDtypeStruct((bsz, h_, dk, dv), F32),
                   jax.ShapeDtypeStruct((bsz, h_, 1, dk), F32),
                   jax.ShapeDtypeStruct((bsz, h_, 1, 1), F32)],
        scratch_shapes=[pltpu.VMEM((dk, dv), F32), pltpu.VMEM((1, dk), F32), pltpu.VMEM((1, 1), F32)],
        compiler_params=_cp("parallel", "parallel", "arbitrary"),
        name="mlstm",
    )(proj, proj, proj, proj, g_row, g_col, norm_g.reshape(1, h_ * dv),
      c0, n0.reshape(bsz, h_, 1, dk), m0.reshape(bsz, h_, 1, 1))
    hg, c_f, n_f, m_f = out
    return hg, c_f, n_f.reshape(bsz, h_, dk), m_f.reshape(bsz, h_)


def rope_tables(pos, hd):
    rot = hd // 4
    half = rot // 2
    inv = ROPE_THETA ** (-jnp.arange(half, dtype=F32) * 2.0 / rot)
    ang = pos.astype(F32)[:, None] * inv[None, :]
    cos, sin = jnp.cos(ang), jnp.sin(ang)
    n = pos.shape[0]
    rest = hd - rot
    t_cos = jnp.concatenate([cos, cos, jnp.ones((n, rest), F32)], -1)
    t_up = jnp.concatenate([-sin, jnp.zeros((n, hd - half), F32)], -1)
    t_dn = jnp.concatenate([jnp.zeros((n, half), F32), sin, jnp.zeros((n, rest), F32)], -1)
    return t_cos, t_up, t_dn


def _rope_kernel(x_ref, c_ref, a_ref, b_ref, o_ref, *, hd, half):
    w = x_ref.shape[-1]
    x = x_ref[...]
    up = pltpu.roll(x, w - half, 1)
    dn = pltpu.roll(x, half, 1)
    c, a, b = c_ref[...], a_ref[...], b_ref[...]
    for h in range(w // hd):
        sl = slice(h * hd, (h + 1) * hd)
        o_ref[:, sl] = x[:, sl] * c + up[:, sl] * a + dn[:, sl] * b


def rope(x, tabs, hd):
    b, s, d = x.shape
    ts = min(256, s)
    w = min(d, 8 * hd)
    tab = pl.BlockSpec((ts, hd), lambda i, j, c: (j, 0))
    blk = pl.BlockSpec((None, ts, w), lambda i, j, c: (i, j, c))
    return pl.pallas_call(
        functools.partial(_rope_kernel, hd=hd, half=hd // 8),
        grid=(b, s // ts, d // w),
        in_specs=[blk, tab, tab, tab],
        out_specs=blk,
        out_shape=jax.ShapeDtypeStruct((b, s, d), F32),
        compiler_params=_cp("parallel", "parallel", "parallel"),
        name="rope",
    )(x, *tabs)


def _top_mask(gate, n_sel):
    lane = lax.broadcasted_iota(jnp.int32, gate.shape, 1).astype(F32)
    sel = jnp.zeros(gate.shape, F32)
    g = gate
    for _ in range(n_sel):
        mx = jnp.max(g, axis=-1, keepdims=True)
        first = jnp.min(jnp.where(g == mx, lane, float(LANES)), axis=-1, keepdims=True)
        hit = (lane == first) & (mx > -jnp.inf)
        sel = jnp.where(hit, 1.0, sel)
        g = jnp.where(hit, -jnp.inf, g)
    return sel


def _moba_prompt_kernel(q_ref, k_ref, v_ref, o_ref, km_ref, *, blk, n_sel, scale):
    qi = pl.program_id(2)
    nb = k_ref.shape[0] // blk
    hd = q_ref.shape[1]

    @pl.when(qi == 0)
    def _():
        km_ref[...] = jnp.zeros(km_ref.shape, F32)
        for n in range(nb):
            km_ref[n:n + 1, :] = jnp.mean(k_ref[n * blk:(n + 1) * blk, :], axis=0, keepdims=True)

    q = q_ref[...]
    qb = q.astype(BF16)
    gate = lax.dot_general(q, km_ref[...], (((1,), (1,)), ((), ())), precision=HIGHEST,
                           preferred_element_type=F32)
    lane = lax.broadcasted_iota(jnp.int32, gate.shape, 1)
    gate = jnp.where(lane < qi, gate, -jnp.inf)
    sel = _top_mask(gate, n_sel)

    def scores(start):
        kb = k_ref[pl.ds(start, blk), :].astype(BF16)
        return lax.dot_general(qb, kb, (((1,), (1,)), ((), ())), preferred_element_type=F32) * scale

    start = pl.multiple_of(qi * blk, blk)
    ri = lax.broadcasted_iota(jnp.int32, (blk, blk), 0)
    ci = lax.broadcasted_iota(jnp.int32, (blk, blk), 1)
    s = jnp.where(ci <= ri, scores(start), NEG)
    m = jnp.max(s, axis=-1, keepdims=True)
    p = jnp.exp(s - m)
    l = jnp.sum(p, axis=-1, keepdims=True)
    acc = jnp.dot(p.astype(BF16), v_ref[pl.ds(start, blk), :].astype(BF16), preferred_element_type=F32)

    def body(n, carry):
        m, l, acc = carry
        st = pl.multiple_of(n * blk, blk)
        picked = jnp.max(jnp.where(lane == n, sel, 0.0), axis=-1, keepdims=True) > 0.0
        s = jnp.where(picked, scores(st), NEG)
        m_new = jnp.maximum(m, jnp.max(s, axis=-1, keepdims=True))
        a = jnp.exp(m - m_new)
        p = jnp.exp(s - m_new)
        l = a * l + jnp.sum(p, axis=-1, keepdims=True)
        acc = a * acc + jnp.dot(p.astype(BF16), v_ref[pl.ds(st, blk), :].astype(BF16),
                                preferred_element_type=F32)
        return m_new, l, acc

    m, l, acc = lax.fori_loop(0, qi, body, (m, l, acc))
    o_ref[...] = (acc / l).astype(o_ref.dtype)


def moba_prompt(q, k, v, bsz, seq, heads):
    hd = q.shape[1] // heads
    blk = MOBA_BLOCK
    assert seq % blk == 0
    nb = seq // blk
    assert nb <= LANES
    return pl.pallas_call(
        functools.partial(_moba_prompt_kernel, blk=blk, n_sel=min(MOBA_TOPK, nb - 1), scale=float(hd) ** -0.5),
        grid=(bsz, heads, nb),
        in_specs=[pl.BlockSpec((blk, hd), lambda b, h, i: (b * nb + i, h)),
                  pl.BlockSpec((seq, hd), lambda b, h, i: (b, h)),
                  pl.BlockSpec((seq, hd), lambda b, h, i: (b, h))],
        out_specs=pl.BlockSpec((blk, hd), lambda b, h, i: (b * nb + i, h)),
        out_shape=jax.ShapeDtypeStruct((bsz * seq, heads * hd), BF16),
        scratch_shapes=[pltpu.VMEM((LANES, hd), F32)],
        compiler_params=_cp("parallel", "parallel", "arbitrary"),
        name="moba_prompt",
    )(q, k, v)


def _page_sum_kernel(pt_ref, k_ref, o_ref):
    o_ref[...] = jnp.sum(k_ref[...], axis=0, keepdims=True)


def page_sums(cache, layer, page_table):
    _, _, page, d = cache.shape
    nreq, npg = page_table.shape
    grid_spec = pltpu.PrefetchScalarGridSpec(
        num_scalar_prefetch=1, grid=(nreq, npg),
        in_specs=[pl.BlockSpec((None, None, page, d), lambda b, p, pt: (layer, pt[b * npg + p], 0, 0))],
        out_specs=pl.BlockSpec((None, None, 1, d), lambda b, p, pt: (b, p, 0, 0)))
    return pl.pallas_call(
        _page_sum_kernel, grid_spec=grid_spec,
        out_shape=jax.ShapeDtypeStruct((nreq, npg, 1, d), F32),
        compiler_params=_cp("parallel", "parallel"),
        name="page_sums",
    )(page_table.reshape(-1), cache).reshape(nreq, npg, d)


def _moba_sample_kernel(pid_ref, ok_ref, q_ref, kn_ref, vn_ref, *rest, n_pg, pg_per_blk, n_new, scale):
    k_refs = rest[:n_pg]
    v_refs = rest[n_pg:2 * n_pg]
    o_ref = rest[2 * n_pg]
    b, h, qi = pl.program_id(0), pl.program_id(1), pl.program_id(2)
    base = ((b * pl.num_programs(1) + h) * pl.num_programs(2) + qi) * (n_pg // pg_per_blk)
    q = q_ref[...]
    rows = lax.broadcasted_iota(jnp.int32, (kn_ref.shape[0], 1), 0)
    s_new = jnp.sum(kn_ref[...] * q, axis=-1, keepdims=True) * scale
    s_new = jnp.where((rows <= qi) & (rows < n_new), s_new, NEG)
    s_pg = []
    for i in range(n_pg):
        s = jnp.sum(k_refs[i][...] * q, axis=-1, keepdims=True) * scale
        s_pg.append(jnp.where(ok_ref[base + i // pg_per_blk] > 0, s, NEG))
    m = jnp.max(s_new, axis=0, keepdims=True)
    for s in s_pg:
        m = jnp.maximum(m, jnp.max(s, axis=0, keepdims=True))
    p = jnp.exp(s_new - m)
    l = jnp.sum(p, axis=0, keepdims=True)
    acc = jnp.sum(p * vn_ref[...], axis=0, keepdims=True)
    for i in range(n_pg):
        p = jnp.exp(s_pg[i] - m)
        l = l + jnp.sum(p, axis=0, keepdims=True)
        acc = acc + jnp.sum(p * v_refs[i][...], axis=0, keepdims=True)
    o_ref[...] = (acc / l).astype(o_ref.dtype)


def moba_sample(q, k_new, v_new, cache_k, cache_v, layer, page_table, heads):
    r, s, d = q.shape
    hd = d // heads
    _, _, page, _ = cache_k.shape
    npg = page_table.shape[1]
    ppb = MOBA_BLOCK // page
    nbp = npg // ppb
    assert MOBA_BLOCK % page == 0 and npg % ppb == 0 and s <= MOBA_BLOCK
    n_sel = min(MOBA_TOPK, nbp)
    ksum = page_sums(cache_k, layer, page_table)
    kmean = ksum.reshape(r, nbp, ppb, heads, hd).sum(2) / float(MOBA_BLOCK)
    gate = jnp.einsum('rqhd,rnhd->rhqn', q.reshape(r, s, heads, hd), kmean, precision=HIGHEST)
    top_v, top_i = lax.top_k(gate, n_sel)
    ok = jnp.isfinite(top_v).astype(jnp.int32).reshape(-1)
    pages = page_table[jnp.arange(r)[:, None, None, None, None],
                       top_i[..., None] * ppb + jnp.arange(ppb)]
    pid = pages.reshape(-1).astype(jnp.int32)
    n_pg = n_sel * ppb
    sp = 8
    pad = lambda t: jnp.pad(t, ((0, 0), (0, sp - s), (0, 0)))
    kn, vn = pad(k_new), pad(v_new)

    def pg_spec(i):
        return pl.BlockSpec((None, None, page, hd),
                            lambda b, h, qi, pid, ok, i=i: (layer, pid[((b * heads + h) * s + qi) * n_pg + i], 0, h))

    grid_spec = pltpu.PrefetchScalarGridSpec(
        num_scalar_prefetch=2, grid=(r, heads, s),
        in_specs=[pl.BlockSpec((None, None, 1, hd), lambda b, h, qi, pid, ok: (b, qi, 0, h)),
                  pl.BlockSpec((None, sp, hd), lambda b, h, qi, pid, ok: (b, 0, h)),
                  pl.BlockSpec((None, sp, hd), lambda b, h, qi, pid, ok: (b, 0, h))]
                 + [pg_spec(i) for i in range(n_pg)] * 2,
        out_specs=pl.BlockSpec((None, None, 1, hd), lambda b, h, qi, pid, ok: (b, qi, 0, h)))
    out = pl.pallas_call(
        functools.partial(_moba_sample_kernel, n_pg=n_pg, pg_per_blk=ppb, n_new=s, scale=float(hd) ** -0.5),
        grid_spec=grid_spec,
        out_shape=jax.ShapeDtypeStruct((r, s, 1, d), BF16),
        compiler_params=_cp("parallel", "parallel", "parallel"),
        name="moba_sample",
    )(pid, ok, q.reshape(r, s, 1, d), kn, vn, *([cache_k] * n_pg), *([cache_v] * n_pg))
    return out.reshape(r, s, d)


def _pool_kernel(u_ref, prev_ref, hist_ref, w_ref, sc_ref, o_ref, ext_ref, *, has_hist, wmax):
    g = pl.program_id(0)
    i = pl.program_id(2)
    ts = u_ref.shape[0]
    u = u_ref[...]
    ext_ref[0:wmax, :] = jnp.where(i == 0, hist_ref[...], prev_ref[...])
    ext_ref[wmax:wmax + ts, :] = u
    win = jnp.left_shift(2, g)
    acc = u
    for j in range(1, wmax):
        acc = acc + jnp.where(j < win, ext_ref[wmax - j:wmax - j + ts, :], 0.0)
    winf = win.astype(F32)
    if has_hist:
        cnt = jnp.full((ts, 1), 1.0, F32) * winf
    else:
        row = lax.broadcasted_iota(jnp.int32, (ts, 1), 0) + i * ts
        cnt = jnp.minimum((row + 1).astype(F32), winf)
    d = (acc / cnt - u).astype(BF16)
    o_ref[...] = jnp.dot(d, w_ref[...].astype(BF16), preferred_element_type=F32) * sc_ref[...]


def pool_mix(u, hist, pool_w, layer, pool_scale):
    b, s, d = u.shape
    ng = pool_w.shape[1]
    gsz = d // ng
    wmax = max(POOL_WINDOWS)
    assert POOL_WINDOWS == tuple(2 << g for g in range(ng)) and hist.shape[1] == wmax
    ts = min(POOL_TS, s)
    prev_arr = u if s >= wmax else hist
    r16 = ts // wmax if s >= wmax else 0
    return pl.pallas_call(
        functools.partial(_pool_kernel, has_hist=s < wmax, wmax=wmax),
        grid=(ng, b, s // ts),
        in_specs=[pl.BlockSpec((None, ts, gsz), lambda g, bi, i: (bi, i, g)),
                  pl.BlockSpec((None, wmax, gsz), lambda g, bi, i: (bi, jnp.maximum(i * r16 - 1, 0), g)),
                  pl.BlockSpec((None, wmax, gsz), lambda g, bi, i: (bi, 0, g)),
                  pl.BlockSpec((None, None, gsz, gsz), lambda g, bi, i: (layer, g, 0, 0)),
                  pl.BlockSpec((1, gsz), lambda g, bi, i: (0, g))],
        out_specs=pl.BlockSpec((None, ts, gsz), lambda g, bi, i: (bi, i, g)),
        out_shape=jax.ShapeDtypeStruct((b, s, d), F32),
        scratch_shapes=[pltpu.VMEM((wmax + ts, gsz), F32)],
        compiler_params=_cp("parallel", "parallel", "parallel"),
        name="pool_mix",
    )(u, prev_arr, hist, pool_w, pool_scale.reshape(1, d))


def _first_of_expert(be_ref, blk):
    return (blk == 0) | (be_ref[blk] != be_ref[jnp.maximum(blk - 1, 0)])


def _moe_gu_kernel(be_ref, nu_ref, x_ref, wg_ref, wu_ref, bg_ref, bu_ref, h_ref, wgb_ref, wub_ref):
    blk = pl.program_id(1)

    @pl.when(_first_of_expert(be_ref, blk))
    def _():
        wgb_ref[...] = wg_ref[...].astype(BF16)
        wub_ref[...] = wu_ref[...].astype(BF16)

    @pl.when(blk < nu_ref[0])
    def _():
        x = x_ref[...]
        g = jnp.dot(x, wgb_ref[...], preferred_element_type=F32) + bg_ref[...]
        up = jnp.dot(x, wub_ref[...], preferred_element_type=F32) + bu_ref[...]
        g = jnp.minimum(g, SWIGLU_LIMIT)
        up = jnp.clip(up, -SWIGLU_LIMIT, SWIGLU_LIMIT)
        h_ref[...] = (g * jax.nn.sigmoid(SWIGLU_ALPHA * g) * (up + 1.0)).astype(h_ref.dtype)


def _moe_down_kernel(be_ref, nu_ref, h_ref, w_ref, b_ref, o_ref, wb_ref):
    blk = pl.program_id(1)

    @pl.when(_first_of_expert(be_ref, blk))
    def _():
        wb_ref[...] = w_ref[...].astype(BF16)

    @pl.when(blk < nu_ref[0])
    def _():
        o_ref[...] = (jnp.dot(h_ref[...], wb_ref[...], preferred_element_type=F32) + b_ref[...]).astype(o_ref.dtype)


def moe_experts(xs, blk_exp, n_used, w_gu, b_gu, w_down, b_down, layer):
    p, d = xs.shape
    ne, _, f2 = w_gu.shape[1:]
    f = f2 // 2
    tm = MOE_TM
    nblk = p // tm
    tn = min(MOE_TN, f)
    tn2 = min(MOE_TN2, d)
    last = lambda blk, nu: jnp.minimum(blk, nu[0] - 1)
    gs1 = pltpu.PrefetchScalarGridSpec(
        num_scalar_prefetch=2, grid=(f // tn, nblk),
        in_specs=[pl.BlockSpec((tm, d), lambda j, i, be, nu: (last(i, nu), 0)),
                  pl.BlockSpec((None, None, d, tn), lambda j, i, be, nu: (layer, be[i], 0, j)),
                  pl.BlockSpec((None, None, d, tn), lambda j, i, be, nu: (layer, be[i], 0, f // tn + j)),
                  pl.BlockSpec((None, None, 1, tn), lambda j, i, be, nu: (layer, be[i], 0, j)),
                  pl.BlockSpec((None, None, 1, tn), lambda j, i, be, nu: (layer, be[i], 0, f // tn + j))],
        out_specs=pl.BlockSpec((tm, tn), lambda j, i, be, nu: (last(i, nu), j)),
        scratch_shapes=[pltpu.VMEM((d, tn), BF16), pltpu.VMEM((d, tn), BF16)])
    b_gu4 = b_gu.reshape(b_gu.shape[0], ne, 1, f2)
    h = pl.pallas_call(
        _moe_gu_kernel, grid_spec=gs1,
        out_shape=jax.ShapeDtypeStruct((p, f), BF16),
        compiler_params=_cp("arbitrary", "arbitrary"),
        name="moe_gate_up",
    )(blk_exp, n_used, xs, w_gu, w_gu, b_gu4, b_gu4)
    gs2 = pltpu.PrefetchScalarGridSpec(
        num_scalar_prefetch=2, grid=(d // tn2, nblk),
        in_specs=[pl.BlockSpec((tm, f), lambda j, i, be, nu: (last(i, nu), 0)),
                  pl.BlockSpec((None, None, f, tn2), lambda j, i, be, nu: (layer, be[i], 0, j)),
                  pl.BlockSpec((None, None, 1, tn2), lambda j, i, be, nu: (layer, be[i], 0, j))],
        out_specs=pl.BlockSpec((tm, tn2), lambda j, i, be, nu: (last(i, nu), j)),
        scratch_shapes=[pltpu.VMEM((f, tn2), BF16)])
    return pl.pallas_call(
        _moe_down_kernel, grid_spec=gs2,
        out_shape=jax.ShapeDtypeStruct((p, d), BF16),
        compiler_params=_cp("arbitrary", "arbitrary"),
        name="moe_down",
    )(blk_exp, n_used, h, w_down, b_down.reshape(b_down.shape[0], ne, 1, d))


def moe_route(logits, n_exp):
    t = logits.shape[0]
    tm = MOE_TM
    top_v, top_i = lax.top_k(logits, TOP_K)
    gates = jax.nn.softmax(top_v, axis=-1)
    a = t * TOP_K
    flat_e = top_i.reshape(-1)
    order = jnp.argsort(flat_e)
    sorted_e = flat_e[order]
    counts = jnp.bincount(flat_e, length=n_exp)
    padded = (counts + tm - 1) // tm * tm
    pad_end = jnp.cumsum(padded)
    pad_start = pad_end - padded
    start = jnp.cumsum(counts) - counts
    dest = (pad_start[sorted_e] + jnp.arange(a) - start[sorted_e]).astype(jnp.int32)
    nblk = -(-a // tm) + n_exp
    slot_tok = jnp.full((nblk * tm,), t, jnp.int32).at[dest].set((order // TOP_K).astype(jnp.int32))
    pos = jnp.zeros((a,), jnp.int32).at[order].set(dest).reshape(t, TOP_K)
    n_used = (pad_end[-1] // tm).astype(jnp.int32)
    blk_exp = jnp.minimum(jnp.searchsorted(pad_end, jnp.arange(nblk) * tm, side='right'), n_exp - 1)
    blk_exp = jnp.where(jnp.arange(nblk) < n_used, blk_exp, blk_exp[n_used - 1]).astype(jnp.int32)
    return gates, slot_tok, pos, blk_exp, n_used.reshape(1)


def _pad_seq(t, n):
    return t if t.shape[1] == n else jnp.pad(t, ((0, 0), (0, n - t.shape[1]), (0, 0)))


def _mlstm_layer(u, gates_pre, w_in, norm_g, w_out, j, carry):
    b, s, d = u.shape
    c0, n0, m0 = carry
    h_, dk, dv = c0.shape[1:]
    L = ML_L if s % ML_L == 0 else -(-s // ML_L_MIN) * ML_L_MIN
    sp = -(-s // L) * L
    if sp != s:
        u = _pad_seq(u, sp)
        padg = jnp.concatenate([jnp.full((b, sp - s, h_), NEG, F32), jnp.full((b, sp - s, h_), -NEG, F32)], -1)
        gates_pre = jnp.concatenate([gates_pre, padg], 1)
    nproj = 2 * h_ * dk + 2 * h_ * dv
    proj = matmul(u.reshape(b * sp, d), w_in, j, 0, nproj, BF16)
    hg, c_f, n_f, m_f = mlstm_core(proj, gates_pre, norm_g, c0, n0, m0, b, sp, L)
    y = matmul(hg, w_out, j, 0, d, F32).reshape(b, sp, d)
    return y[:, :s], (c_f, n_f, m_f)


def _moba_layer(u, w_qkv, w_out, j, heads, past):
    b, s, d = u.shape
    hd = d // heads
    x2 = u.reshape(b * s, d)
    q, k, v = (matmul(x2, w_qkv, j, c * d, d, F32).reshape(b, s, d) for c in range(3))
    p0 = 0 if past is None else past[2].shape[1] * past[0].shape[2]
    tabs = rope_tables(p0 + jnp.arange(s), hd)
    q, k = rope(q, tabs, hd), rope(k, tabs, hd)
    if past is None:
        o = moba_prompt(q.reshape(b * s, d), k.reshape(b * s, d), v.reshape(b * s, d), b, s, heads)
    else:
        cache_k, cache_v, page_table = past
        o = moba_sample(q, k, v, cache_k, cache_v, j, page_table, heads).reshape(b * s, d)
    y = matmul(o, w_out, j, 0, d, F32).reshape(b, s, d)
    return y, (k.reshape(b, s, heads, hd), v.reshape(b, s, heads, hd))


def kernel(x_prompt, x_sample, c_prompt, c_sample, state_mlstm_C, state_mlstm_n, state_mlstm_m, cache_moba_k, cache_moba_v, page_table, state_pool, ada_w, ada_b, ln_g, ln_b, mlstm_w_in, mlstm_b_gates, mlstm_norm_g, mlstm_w_out, moba_w_qkv, moba_w_out, pool_w, pool_scale, router_w, router_b, moe_w_gu, moe_b_gu, moe_w_down, moe_b_down):
    depth, d = ada_w.shape[0], ada_w.shape[1]
    bp, sp_, _ = x_prompt.shape
    bs, ss, _ = x_sample.shape
    n_exp = router_w.shape[2]
    ml_heads, ml_dk, ml_dv = state_mlstm_C.shape[2:]
    moba_heads = cache_moba_k.shape[3]
    alpha = (2.0 * depth) ** 0.25
    page = cache_moba_k.shape[2]
    cache_k = cache_moba_k.reshape(cache_moba_k.shape[:3] + (-1,))
    cache_v = cache_moba_v.reshape(cache_moba_v.shape[:3] + (-1,))
    nproj = 2 * ml_heads * (ml_dk + ml_dv)

    nreq = bp + bs
    rpad = -(-nreq // 8) * 8
    c_all = jnp.pad(jnp.concatenate([c_prompt, c_sample], 0), ((0, rpad - nreq), (0, 0)))
    mods_all = ada_mod(c_all, ada_w, ada_b)
    groups = [dict(x=x_prompt, lo=0, b=bp), dict(x=x_sample, lo=bp, b=bs)]

    def mods_of(i, grp):
        return mods_all[i, grp['lo']:grp['lo'] + grp['b']].reshape(grp['b'], 1, 6 * d)

    def mixer_aux(i):
        if i % N_MIXERS == 0:
            jj = i // N_MIXERS
            return mlstm_w_in[jj][:, nproj:], mlstm_b_gates[jj]
        return None, None

    def u_dtype(i):
        return F32 if i % N_MIXERS == 2 else BF16

    aw, ab = mixer_aux(0)
    for grp in groups:
        outs = ln_mod(grp['x'], mods_of(0, grp), mod_c=(0, 1), aux_w=aw, aux_b=ab, u_dtype=u_dtype(0))
        grp['u'] = outs[0]
        grp['aux'] = outs[1] if aw is not None else None

    ml, mb, plst = ([], []), ([], []), ([], [])
    for i in range(depth):
        kind, j = i % N_MIXERS, i // N_MIXERS
        ys = []
        for gi, grp in enumerate(groups):
            u = grp['u']
            b, s, _ = u.shape
            if kind == 0:
                if gi == 0:
                    carry = (jnp.zeros((b, ml_heads, ml_dk, ml_dv), F32), jnp.zeros((b, ml_heads, ml_dk), F32),
                             jnp.full((b, ml_heads), ML_M_INIT, F32))
                else:
                    carry = (state_mlstm_C[j].astype(F32), state_mlstm_n[j].astype(F32), state_mlstm_m[j].astype(F32))
                y, st = _mlstm_layer(u, grp['aux'], mlstm_w_in, mlstm_norm_g[j], mlstm_w_out, j, carry)
                ml[gi].append(st)
            elif kind == 1:
                past = None if gi == 0 else (cache_k, cache_v, page_table)
                y, st = _moba_layer(u, moba_w_qkv, moba_w_out, j, moba_heads, past)
                mb[gi].append(st)
            else:
                wmax = max(POOL_WINDOWS)
                if gi == 0:
                    hist = jnp.zeros((b, wmax, d), F32)
                    ext = u
                else:
                    hist = jnp.pad(state_pool[j].astype(F32), ((0, 0), (1, 0), (0, 0)))
                    ext = jnp.concatenate([state_pool[j].astype(F32), u], 1)
                y = pool_mix(u, hist, pool_w, j, pool_scale[j])
                plst[gi].append(ext[:, ext.shape[1] - (wmax - 1):])
            ys.append(y)

        for grp, y in zip(groups, ys):
            grp['x'], grp['u'], grp['aux'] = ln_mod(
                grp['x'], mods_of(i, grp), y=y, gate_c=2, ln_g=ln_g[i, 0], ln_b=ln_b[i, 0], mod_c=(3, 4),
                aux_w=router_w[i], aux_b=router_b[i], u_dtype=BF16, alpha=alpha)

        toks = [g_['u'].reshape(-1, d) for g_ in groups]
        logits = jnp.concatenate([g_['aux'].reshape(-1, n_exp) for g_ in groups], 0)
        t_all = logits.shape[0]
        gates, slot_tok, pos, blk_exp, n_used = moe_route(logits, n_exp)
        x_pad = jnp.concatenate(toks + [jnp.zeros((1, d), BF16)], 0)
        xs = jnp.take(x_pad, slot_tok, axis=0)
        ys_slots = moe_experts(xs, blk_exp, n_used, moe_w_gu, moe_b_gu, moe_w_down, moe_b_down, i)

        last = i == depth - 1
        aw, ab = (None, None) if last else mixer_aux(i + 1)
        t0 = 0
        for grp in groups:
            b, s, _ = grp['x'].shape
            n = b * s
            pg = pos[t0:t0 + n]
            ysg = jnp.take(ys_slots, pg.T.reshape(-1), axis=0).reshape(TOP_K, b, s, d)
            gt = gates[t0:t0 + n].reshape(b, s, TOP_K)
            t0 += n
            outs = ln_mod(grp['x'], mods_of(i, grp) if last else jnp.concatenate([mods_of(i, grp), mods_of(i + 1, grp)], -1),
                          ys=ysg, gates=gt, gate_c=5, ln_g=ln_g[i, 1], ln_b=ln_b[i, 1],
                          mod_c=None if last else (6, 7), aux_w=aw, aux_b=ab,
                          u_dtype=BF16 if last else u_dtype(i + 1), alpha=alpha)
            grp['x'] = outs[0]
            if not last:
                grp['u'] = outs[1]
                grp['aux'] = outs[2] if aw is not None else None

    xdt = x_prompt.dtype
    sdt = state_mlstm_C.dtype
    stack = lambda sts, k, dt: jnp.stack([s_[k] for s_ in sts]).astype(dt)
    return (groups[0]['x'], groups[1]['x'],
            stack(ml[0], 0, xdt), stack(ml[0], 1, xdt), stack(ml[0], 2, xdt),
            stack(ml[1], 0, sdt), stack(ml[1], 1, sdt), stack(ml[1], 2, sdt),
            stack(mb[0], 0, xdt), stack(mb[0], 1, xdt), stack(mb[1], 0, xdt), stack(mb[1], 1, xdt),
            jnp.stack(plst[0]), jnp.stack(plst[1]))
```

```python
import functools

import jax
import jax.numpy as jnp
from jax import lax
from jax.experimental import pallas as pl
from jax.experimental.pallas import tpu as pltpu

F32 = jnp.float32
BF16 = jnp.bfloat16

N_MIXERS = 3
ML_M_INIT = -1.0e30
ML_CHUNK = 128
MOBA_BLOCK = 256
MOBA_TOPK = 3
ROPE_THETA = 500000.0
POOL_WINDOWS = (2, 4, 8, 16)
TOP_K = 4
SWIGLU_LIMIT = 7.0
SWIGLU_ALPHA = 1.702
LN_EPS = 1e-5
RMS_EPS = 1e-6

LANES = 128
VMEM_LIMIT = 56 * 1024 * 1024
MM_TM = 1024
MM_TN = 512
LN_TS = 128
MOE_TM = 256
MOE_TN = 512
MOE_TN2 = 2048
ML_L = ML_CHUNK
ML_L_MIN = 16
POOL_TS = 256
NEG = -1.0e30


def _cp(*sem):
    return pltpu.CompilerParams(dimension_semantics=sem, vmem_limit_bytes=VMEM_LIMIT)


def _ada_kernel(c_ref, w_ref, b_ref, o_ref):
    c = c_ref[...]
    a = (c * jax.nn.sigmoid(c)).astype(BF16)
    o_ref[...] = jnp.dot(a, w_ref[...].astype(BF16), preferred_element_type=F32) + b_ref[...]


def ada_mod(c_all, ada_w, ada_b):
    depth, d, n = ada_w.shape
    r = c_all.shape[0]
    tn = min(MM_TN, n)
    return pl.pallas_call(
        _ada_kernel,
        grid=(depth, n // tn),
        in_specs=[pl.BlockSpec((r, d), lambda l, j: (0, 0)),
                  pl.BlockSpec((None, d, tn), lambda l, j: (l, 0, j)),
                  pl.BlockSpec((None, 1, tn), lambda l, j: (l, 0, j))],
        out_specs=pl.BlockSpec((None, r, tn), lambda l, j: (l, 0, j)),
        out_shape=jax.ShapeDtypeStruct((depth, r, n), F32),
        compiler_params=_cp("arbitrary", "arbitrary"),
        name="ada_mod",
    )(c_all, ada_w, ada_b.reshape(depth, 1, n))


def _ln_mod_kernel(*refs, res, has_ln, has_mod, has_aux, alpha):
    it = iter(refs)
    x_ref = next(it)
    x = x_ref[...]
    if has_ln:
        if res == "dense":
            y = next(it)[...].astype(F32)
        else:
            ys_ref = next(it)
            gt = next(it)[...]
            y = gt[:, 0:1] * ys_ref[0].astype(F32)
            for k in range(1, ys_ref.shape[0]):
                y = y + gt[:, k:k + 1] * ys_ref[k].astype(F32)
        gate = next(it)[...]
        lg = next(it)[...]
        lb = next(it)[...]
        z = alpha * x + gate * y
        mu = jnp.mean(z, axis=-1, keepdims=True)
        zc = z - mu
        var = jnp.mean(zc * zc, axis=-1, keepdims=True)
        x = zc * lax.rsqrt(var + LN_EPS) * lg + lb
    if has_mod:
        sc = next(it)[...]
        sh = next(it)[...]
        u = x * (1.0 + sc) + sh
    if has_aux:
        wa = next(it)[...]
        ba = next(it)[...]
    if has_ln:
        next(it)[...] = x
    if has_mod:
        u_ref = next(it)
        u_ref[...] = u.astype(u_ref.dtype)
    if has_aux:
        next(it)[...] = jnp.dot(u.astype(BF16), wa.astype(BF16), preferred_element_type=F32) + ba


def ln_mod(x, mods, *, y=None, ys=None, gates=None, gate_c=None, ln_g=None, ln_b=None,
           mod_c=None, aux_w=None, aux_b=None, u_dtype=BF16, alpha=1.0):
    b, s, d = x.shape
    ts = min(LN_TS, s)
    grid = (b, s // ts)
    row = pl.BlockSpec((None, ts, d), lambda i, j: (i, j, 0))

    def modspec(c):
        return pl.BlockSpec((None, 1, d), lambda i, j, c=c: (i, 0, c))

    vec = pl.BlockSpec((1, d), lambda i, j: (0, 0))
    has_ln = gate_c is not None
    has_mod = mod_c is not None
    has_aux = aux_w is not None
    res = "dense" if y is not None else "moe"
    args, specs = [x], [row]
    if has_ln:
        if y is not None:
            args.append(y)
            specs.append(row)
        else:
            k = ys.shape[0]
            args += [ys, gates]
            specs += [pl.BlockSpec((k, None, ts, d), lambda i, j: (0, i, j, 0)),
                      pl.BlockSpec((None, ts, k), lambda i, j: (i, j, 0))]
        args += [mods, ln_g.reshape(1, d), ln_b.reshape(1, d)]
        specs += [modspec(gate_c), vec, vec]
    if has_mod:
        args += [mods, mods]
        specs += [modspec(mod_c[1]), modspec(mod_c[0])]
    if has_aux:
        na = aux_w.shape[1]
        args += [aux_w, aux_b.reshape(1, na)]
        specs += [pl.BlockSpec((d, na), lambda i, j: (0, 0)), pl.BlockSpec((1, na), lambda i, j: (0, 0))]
    out_shape, out_specs = [], []
    if has_ln:
        out_shape.append(jax.ShapeDtypeStruct((b, s, d), F32))
        out_specs.append(row)
    if has_mod:
        out_shape.append(jax.ShapeDtypeStruct((b, s, d), u_dtype))
        out_specs.append(row)
    if has_aux:
        out_shape.append(jax.ShapeDtypeStruct((b, s, na), F32))
        out_specs.append(pl.BlockSpec((None, ts, na), lambda i, j: (i, j, 0)))
    return pl.pallas_call(
        functools.partial(_ln_mod_kernel, res=res, has_ln=has_ln, has_mod=has_mod, has_aux=has_aux, alpha=alpha),
        grid=grid, in_specs=specs, out_specs=out_specs, out_shape=out_shape,
        compiler_params=_cp("parallel", "parallel"),
        name="ln_mod",
    )(*args)


def _mm_kernel(x_ref, w_ref, o_ref, wb_ref):
    @pl.when(pl.program_id(1) == 0)
    def _():
        wb_ref[...] = w_ref[...].astype(BF16)

    o_ref[...] = jnp.dot(x_ref[...], wb_ref[...], preferred_element_type=F32).astype(o_ref.dtype)


def matmul(x, w, layer, col0, n, out_dtype):
    m, k = x.shape
    tm = min(MM_TM, m)
    tn = min(MM_TN, n)
    assert m % tm == 0 and n % tn == 0 and col0 % tn == 0
    c0 = col0 // tn
    return pl.pallas_call(
        _mm_kernel,
        grid=(n // tn, m // tm),
        in_specs=[pl.BlockSpec((tm, k), lambda j, i: (i, 0)),
                  pl.BlockSpec((None, k, tn), lambda j, i: (layer, 0, c0 + j))],
        out_specs=pl.BlockSpec((tm, tn), lambda j, i: (i, j)),
        out_shape=jax.ShapeDtypeStruct((m, n), out_dtype),
        scratch_shapes=[pltpu.VMEM((k, tn), BF16)],
        compiler_params=_cp("arbitrary", "arbitrary"),
        name="matmul",
    )(x, w)


def _split3(x):
    hi = x.astype(BF16)
    r1 = x - hi.astype(F32)
    mid = r1.astype(BF16)
    lo = (r1 - mid.astype(F32)).astype(BF16)
    return hi, mid, lo


def _log_sigmoid(x):
    return jnp.minimum(x, 0.0) - jnp.log(1.0 + jnp.exp(-jnp.abs(x)))


def _mlstm_kernel(q_ref, k_ref, v_ref, o_ref, gr_ref, gc_ref, ng_ref, c0_ref, n0_ref, m0_ref,
                  h_ref, co_ref, no_ref, mo_ref, c_sc, n_sc, m_sc, *, scale):
    c_idx = pl.program_id(2)
    nc = pl.num_programs(2)
    L = q_ref.shape[0]

    @pl.when(c_idx == 0)
    def _():
        c_sc[...] = c0_ref[...]
        n_sc[...] = n0_ref[...]
        m_sc[...] = m0_ref[...]

    q = q_ref[...]
    k = k_ref[...]
    v = v_ref[...]
    gr = gr_ref[...]
    gc = gc_ref[...]
    logi_r = gr[0:1, :]
    logf_r = _log_sigmoid(gr[1:2, :])
    logi_c = gc[:, 0:1]
    logf_c = _log_sigmoid(gc[:, 1:2])
    ri = lax.broadcasted_iota(jnp.int32, (L, L), 0)
    ci = lax.broadcasted_iota(jnp.int32, (L, L), 1)
    causal = ci <= ri
    upper = jnp.where(ri <= ci, 1.0, 0.0).astype(BF16)
    lower = jnp.where(ci <= ri, 1.0, 0.0).astype(BF16)
    fr = jnp.broadcast_to(logf_r, (8, L))
    fc = jnp.broadcast_to(logf_c, (L, LANES))
    b_r = sum(jnp.dot(p, upper, preferred_element_type=F32) for p in _split3(fr))[0:1, :]
    b_c = sum(jnp.dot(lower, p, preferred_element_type=F32) for p in _split3(fc))[:, 0:1]
    m_prev = m_sc[...]
    logw = jnp.where(causal, b_c - b_r + logi_r, -jnp.inf)
    m_inter = m_prev + b_c
    m_t = jnp.maximum(m_inter, jnp.max(logw, axis=-1, keepdims=True))
    s = lax.dot_general(q, k, (((1,), (1,)), ((), ())), preferred_element_type=F32)
    s = s * scale * jnp.exp(logw - m_t)
    inter = jnp.exp(m_inter - m_t)
    c_mat = c_sc[...]
    n_vec = n_sc[...]
    qf = q.astype(F32) * scale
    num = jnp.dot(s.astype(BF16), v, preferred_element_type=F32) + inter * (
        jnp.dot(q, c_mat.astype(BF16), preferred_element_type=F32) * scale)
    den = jnp.sum(s, axis=-1, keepdims=True) + inter * jnp.sum(
        qf * n_vec.astype(BF16).astype(F32), axis=-1, keepdims=True)
    h = num / jnp.maximum(jnp.abs(den), jnp.exp(-m_t))
    m_new = m_t[L - 1:L, :]
    b_last = b_c[L - 1:L, :]
    w_end = jnp.exp(b_last - b_c + logi_c - m_new)
    decay = jnp.exp(m_prev + b_last - m_new)
    kw = k.astype(F32) * w_end
    c_new = decay * c_mat + lax.dot_general(kw.astype(BF16), v, (((0,), (0,)), ((), ())),
                                            preferred_element_type=F32)
    n_new = decay * n_vec + jnp.sum(kw, axis=0, keepdims=True)
    c_sc[...] = c_new
    n_sc[...] = n_new
    m_sc[...] = m_new
    hn = h * lax.rsqrt(jnp.mean(h * h, axis=-1, keepdims=True) + RMS_EPS) * ng_ref[...]
    h_ref[...] = (jax.nn.sigmoid(o_ref[...].astype(F32)) * hn).astype(h_ref.dtype)

    @pl.when(c_idx == nc - 1)
    def _():
        co_ref[...] = c_new
        no_ref[...] = n_new
        mo_ref[...] = m_new


def mlstm_core(proj, ogate, gates, norm_g, c0, n0, m0, bsz, seq, L):
    h_, dk, dv = c0.shape[1:]
    nc = seq // L
    g = gates.reshape(bsz, nc, L, 2, h_)
    g_row = g.transpose(0, 4, 1, 3, 2)
    g_col = g.transpose(0, 4, 1, 2, 3)
    kq, kk, kv = 0, h_ * dk // dk, 2 * h_ * dk // dv
    row = lambda b, h, c: b * nc + c
    out = pl.pallas_call(
        functools.partial(_mlstm_kernel, scale=float(dk) ** -0.5),
        grid=(bsz, h_, nc),
        in_specs=[pl.BlockSpec((L, dk), lambda b, h, c: (row(b, h, c), kq + h)),
                  pl.BlockSpec((L, dk), lambda b, h, c: (row(b, h, c), kk + h)),
                  pl.BlockSpec((L, dv), lambda b, h, c: (row(b, h, c), kv + h)),
                  pl.BlockSpec((L, dv), lambda b, h, c: (row(b, h, c), h)),
                  pl.BlockSpec((None, None, None, 2, L), lambda b, h, c: (b, h, c, 0, 0)),
                  pl.BlockSpec((None, None, None, L, 2), lambda b, h, c: (b, h, c, 0, 0)),
                  pl.BlockSpec((1, dv), lambda b, h, c: (0, h)),
                  pl.BlockSpec((None, None, dk, dv), lambda b, h, c: (b, h, 0, 0)),
                  pl.BlockSpec((None, None, 1, dk), lambda b, h, c: (b, h, 0, 0)),
                  pl.BlockSpec((None, None, 1, 1), lambda b, h, c: (b, h, 0, 0))],
        out_specs=[pl.BlockSpec((L, dv), lambda b, h, c: (row(b, h, c), h)),
                   pl.BlockSpec((None, None, dk, dv), lambda b, h, c: (b, h, 0, 0)),
                   pl.BlockSpec((None, None, 1, dk), lambda b, h, c: (b, h, 0, 0)),
                   pl.BlockSpec((None, None, 1, 1), lambda b, h, c: (b, h, 0, 0))],
        out_shape=[jax.ShapeDtypeStruct((bsz * seq, h_ * dv), BF16),
                   jax.ShapeDtypeStruct((bsz, h_, dk, dv), F32),
                   jax.ShapeDtypeStruct((bsz, h_, 1, dk), F32),
                   jax.ShapeDtypeStruct((bsz, h_, 1, 1), F32)],
        scratch_shapes=[pltpu.VMEM((dk, dv), F32), pltpu.VMEM((1, dk), F32), pltpu.VMEM((1, 1), F32)],
        compiler_params=_cp("parallel", "parallel", "arbitrary"),
        name="mlstm",
    )(proj, proj, proj, ogate, g_row, g_col, norm_g.reshape(1, h_ * dv),
      c0, n0.reshape(bsz, h_, 1, dk), m0.reshape(bsz, h_, 1, 1))
    hg, c_f, n_f, m_f = out
    return hg, c_f, n_f.reshape(bsz, h_, dk), m_f.reshape(bsz, h_)


def rope_tables(pos, hd):
    rot = hd // 4
    half = rot // 2
    inv = ROPE_THETA ** (-jnp.arange(half, dtype=F32) * 2.0 / rot)
    ang = pos.astype(F32)[:, None] * inv[None, :]
    cos, sin = jnp.cos(ang), jnp.sin(ang)
    n = pos.shape[0]
    rest = hd - rot
    t_cos = jnp.concatenate([cos, cos, jnp.ones((n, rest), F32)], -1)
    t_up = jnp.concatenate([-sin, jnp.zeros((n, hd - half), F32)], -1)
    t_dn = jnp.concatenate([jnp.zeros((n, half), F32), sin, jnp.zeros((n, rest), F32)], -1)
    return t_cos, t_up, t_dn


def _rope_kernel(x_ref, c_ref, a_ref, b_ref, o_ref, *, hd, half):
    w = x_ref.shape[-1]
    x = x_ref[...]
    up = pltpu.roll(x, w - half, 1)
    dn = pltpu.roll(x, half, 1)
    c, a, b = c_ref[...], a_ref[...], b_ref[...]
    for h in range(w // hd):
        sl = slice(h * hd, (h + 1) * hd)
        o_ref[:, sl] = x[:, sl] * c + up[:, sl] * a + dn[:, sl] * b


def rope(x, tabs, hd):
    b, s, d = x.shape
    ts = min(256, s)
    w = min(d, 8 * hd)
    tab = pl.BlockSpec((ts, hd), lambda i, j, c: (j, 0))
    blk = pl.BlockSpec((None, ts, w), lambda i, j, c: (i, j, c))
    return pl.pallas_call(
        functools.partial(_rope_kernel, hd=hd, half=hd // 8),
        grid=(b, s // ts, d // w),
        in_specs=[blk, tab, tab, tab],
        out_specs=blk,
        out_shape=jax.ShapeDtypeStruct((b, s, d), F32),
        compiler_params=_cp("parallel", "parallel", "parallel"),
        name="rope",
    )(x, *tabs)


def _top_rows(gate, n_sel):
    row = lax.broadcasted_iota(jnp.int32, gate.shape, 0).astype(F32)
    sel = jnp.zeros(gate.shape, F32)
    g = gate
    for _ in range(n_sel):
        mx = jnp.max(g, axis=0, keepdims=True)
        first = jnp.min(jnp.where(g == mx, row, float(gate.shape[0])), axis=0, keepdims=True)
        hit = (row == first) & (mx > -jnp.inf)
        sel = jnp.where(hit, 1.0, sel)
        g = jnp.where(hit, -jnp.inf, g)
    return sel


def _moba_prompt_kernel(q_ref, k_ref, v_ref, o_ref, km_ref, kb_ref, vt_ref, sel_ref, *, n_sel, scale):
    qi = pl.program_id(2)
    nb, blk, _ = kb_ref.shape

    @pl.when(qi == 0)
    def _():
        km_ref[...] = jnp.zeros(km_ref.shape, F32)
        for n in range(nb):
            kblk = k_ref[n * blk:(n + 1) * blk, :]
            km_ref[n:n + 1, :] = jnp.mean(kblk, axis=0, keepdims=True)
            kb_ref[n] = kblk.astype(BF16)
            vt_ref[n] = v_ref[n * blk:(n + 1) * blk, :].T.astype(BF16)

    qt = q_ref[...].T
    qtb = qt.astype(BF16)
    gate = jnp.dot(km_ref[...].astype(BF16), qtb, preferred_element_type=F32)
    row = lax.broadcasted_iota(jnp.int32, gate.shape, 0)
    sel_ref[...] = _top_rows(jnp.where(row < qi, gate, -jnp.inf), n_sel)
    ki = lax.broadcasted_iota(jnp.int32, (blk, blk), 0)
    ci = lax.broadcasted_iota(jnp.int32, (blk, blk), 1)
    s = jnp.where(ki <= ci, jnp.dot(kb_ref[qi], qtb, preferred_element_type=F32) * scale, NEG)
    m = jnp.max(s, axis=0, keepdims=True)
    p = jnp.exp(s - m)
    l = jnp.sum(p, axis=0, keepdims=True)
    acc = jnp.dot(vt_ref[qi], p.astype(BF16), preferred_element_type=F32)

    def body(n, carry):
        m, l, acc = carry
        picked = sel_ref[pl.ds(n, 1), :] > 0.0
        s = jnp.where(picked, jnp.dot(kb_ref[n], qtb, preferred_element_type=F32) * scale, NEG)
        m_new = jnp.maximum(m, jnp.max(s, axis=0, keepdims=True))
        a = jnp.exp(m - m_new)
        p = jnp.exp(s - m_new)
        l = a * l + jnp.sum(p, axis=0, keepdims=True)
        acc = a * acc + jnp.dot(vt_ref[n], p.astype(BF16), preferred_element_type=F32)
        return m_new, l, acc

    m, l, acc = lax.fori_loop(0, qi, body, (m, l, acc))
    o_ref[...] = (acc / l).T.astype(o_ref.dtype)


def moba_prompt(q, k, v, bsz, seq, heads):
    hd = q.shape[1] // heads
    blk = MOBA_BLOCK
    assert seq % blk == 0
    nb = seq // blk
    nbp = -(-nb // 8) * 8
    return pl.pallas_call(
        functools.partial(_moba_prompt_kernel, n_sel=min(MOBA_TOPK, nb - 1), scale=float(hd) ** -0.5),
        grid=(bsz, heads, nb),
        in_specs=[pl.BlockSpec((blk, hd), lambda b, h, i: (b * nb + i, h)),
                  pl.BlockSpec((seq, hd), lambda b, h, i: (b, h)),
                  pl.BlockSpec((seq, hd), lambda b, h, i: (b, h))],
        out_specs=pl.BlockSpec((blk, hd), lambda b, h, i: (b * nb + i, h)),
        out_shape=jax.ShapeDtypeStruct((bsz * seq, heads * hd), BF16),
        scratch_shapes=[pltpu.VMEM((nbp, hd), F32), pltpu.VMEM((nb, blk, hd), BF16),
                        pltpu.VMEM((nb, hd, blk), BF16), pltpu.VMEM((nbp, blk), F32)],
        compiler_params=_cp("parallel", "parallel", "arbitrary"),
        name="moba_prompt",
    )(q, k, v)


def _page_sum_kernel(pt_ref, k_ref, o_ref):
    o_ref[...] = jnp.sum(k_ref[...], axis=0)


def page_sums(cache, layer, page_table):
    _, _, page, heads, hd = cache.shape
    nreq, npg = page_table.shape
    grid_spec = pltpu.PrefetchScalarGridSpec(
        num_scalar_prefetch=1, grid=(nreq, npg),
        in_specs=[pl.BlockSpec((None, None, page, heads, hd),
                               lambda b, p, pt: (layer, pt[b * npg + p], 0, 0, 0))],
        out_specs=pl.BlockSpec((None, None, heads, hd), lambda b, p, pt: (b, p, 0, 0)))
    return pl.pallas_call(
        _page_sum_kernel, grid_spec=grid_spec,
        out_shape=jax.ShapeDtypeStruct((nreq, npg, heads, hd), F32),
        compiler_params=_cp("parallel", "parallel"),
        name="page_sums",
    )(page_table.reshape(-1), cache)


def _moba_sample_kernel(pid_ref, ok_ref, q_ref, kn_ref, vn_ref, ck_ref, cv_ref, o_ref, kbuf, vbuf, sem,
                        *, layer, n_sel, ppb, scale):
    nh = pl.num_programs(1)
    step = pl.program_id(0) * nh + pl.program_id(1)
    nsteps = pl.num_programs(0) * nh
    n_q = q_ref.shape[0]
    n_pg = n_sel * ppb
    per = n_q * n_pg

    def copies(st, slot):
        hh = st % nh
        out = []
        for j in range(per):
            pg = pid_ref[st * per + j]
            out.append(pltpu.make_async_copy(ck_ref.at[layer, pg, :, hh, :], kbuf.at[slot, j], sem.at[0, slot]))
            out.append(pltpu.make_async_copy(cv_ref.at[layer, pg, :, hh, :], vbuf.at[slot, j], sem.at[1, slot]))
        return out

    @pl.when(step == 0)
    def _():
        for c in copies(step, 0):
            c.start()

    @pl.when(step + 1 < nsteps)
    def _():
        for c in copies(step + 1, (step + 1) % 2):
            c.start()

    slot = step % 2
    for c in copies(step, slot):
        c.wait()

    rows = lax.broadcasted_iota(jnp.int32, (kn_ref.shape[0], 1), 0)
    rnd = lambda t: t.astype(BF16).astype(F32)
    kn = rnd(kn_ref[...])
    vn = rnd(vn_ref[...])
    for qq in range(n_q):
        q = rnd(q_ref[qq:qq + 1, :])
        s_new = jnp.sum(kn * q, axis=-1, keepdims=True) * scale
        s_new = jnp.where(rows <= qq, s_new, NEG)
        s_pg = []
        for i in range(n_pg):
            s = jnp.sum(rnd(kbuf[slot, qq * n_pg + i]) * q, axis=-1, keepdims=True) * scale
            s_pg.append(jnp.where(ok_ref[(step * n_q + qq) * n_sel + i // ppb] > 0, s, NEG))
        m = jnp.max(s_new, axis=0, keepdims=True)
        for s in s_pg:
            m = jnp.maximum(m, jnp.max(s, axis=0, keepdims=True))
        p = jnp.exp(s_new - m)
        l = jnp.sum(p, axis=0, keepdims=True)
        acc = jnp.sum(rnd(p) * vn, axis=0, keepdims=True)
        for i in range(n_pg):
            p = jnp.exp(s_pg[i] - m)
            l = l + jnp.sum(p, axis=0, keepdims=True)
            acc = acc + jnp.sum(rnd(p) * rnd(vbuf[slot, qq * n_pg + i]), axis=0, keepdims=True)
        o_ref[qq:qq + 1, :] = (acc / l).astype(o_ref.dtype)


def moba_sample(q, k_new, v_new, cache_k, cache_v, layer, page_table, heads):
    r, s, d = q.shape
    hd = d // heads
    page = cache_k.shape[2]
    npg = page_table.shape[1]
    ppb = MOBA_BLOCK // page
    nbp = npg // ppb
    sp = 8
    assert MOBA_BLOCK % page == 0 and npg % ppb == 0 and s <= sp
    n_sel = min(MOBA_TOPK, nbp)
    ksum = page_sums(cache_k, layer, page_table)
    kmean = ksum.reshape(r, nbp, ppb, heads, hd).sum(2) / float(MOBA_BLOCK)
    gate = jnp.einsum('rqhd,rnhd->rhqn', q.reshape(r, s, heads, hd).astype(BF16), kmean.astype(BF16),
                      preferred_element_type=F32)
    top_v, top_i = lax.top_k(gate, n_sel)
    ok = jnp.isfinite(top_v).astype(jnp.int32).reshape(-1)
    pages = page_table[jnp.arange(r)[:, None, None, None, None],
                       top_i[..., None] * ppb + jnp.arange(ppb)]
    pid = pages.reshape(-1).astype(jnp.int32)
    n_pg = n_sel * ppb
    pad = lambda t: jnp.pad(t, ((0, 0), (0, sp - s), (0, 0)))
    head_blk = lambda rows: pl.BlockSpec((None, rows, hd), lambda b, h, pid, ok: (b, 0, h))
    grid_spec = pltpu.PrefetchScalarGridSpec(
        num_scalar_prefetch=2, grid=(r, heads),
        in_specs=[head_blk(s), head_blk(sp), head_blk(sp),
                  pl.BlockSpec(memory_space=pl.ANY), pl.BlockSpec(memory_space=pl.ANY)],
        out_specs=head_blk(s),
        scratch_shapes=[pltpu.VMEM((2, s * n_pg, page, hd), F32), pltpu.VMEM((2, s * n_pg, page, hd), F32),
                        pltpu.SemaphoreType.DMA((2, 2))])
    return pl.pallas_call(
        functools.partial(_moba_sample_kernel, layer=layer, n_sel=n_sel, ppb=ppb, scale=float(hd) ** -0.5),
        grid_spec=grid_spec,
        out_shape=jax.ShapeDtypeStruct((r, s, d), BF16),
        compiler_params=_cp("arbitrary", "arbitrary"),
        name="moba_sample",
    )(pid, ok, q, pad(k_new), pad(v_new), cache_k, cache_v)


def _pool_kernel(u_ref, prev_ref, hist_ref, w_ref, sc_ref, o_ref, ext_ref, *, has_hist, wmax):
    g = pl.program_id(0)
    i = pl.program_id(2)
    ts = u_ref.shape[0]
    u = u_ref[...]
    ext_ref[0:wmax, :] = jnp.where(i == 0, hist_ref[...], prev_ref[...])
    ext_ref[wmax:wmax + ts, :] = u
    win = jnp.left_shift(2, g)
    acc = u
    for j in range(1, wmax):
        acc = acc + jnp.where(j < win, ext_ref[wmax - j:wmax - j + ts, :], 0.0)
    winf = win.astype(F32)
    if has_hist:
        cnt = jnp.full((ts, 1), 1.0, F32) * winf
    else:
        row = lax.broadcasted_iota(jnp.int32, (ts, 1), 0) + i * ts
        cnt = jnp.minimum((row + 1).astype(F32), winf)
    d = (acc / cnt - u).astype(BF16)
    o_ref[...] = jnp.dot(d, w_ref[...].astype(BF16), preferred_element_type=F32) * sc_ref[...]


def pool_mix(u, hist, pool_w, layer, pool_scale):
    b, s, d = u.shape
    ng = pool_w.shape[1]
    gsz = d // ng
    wmax = max(POOL_WINDOWS)
    assert POOL_WINDOWS == tuple(2 << g for g in range(ng)) and hist.shape[1] == wmax
    ts = min(POOL_TS, s)
    prev_arr = u if s >= wmax else hist
    r16 = ts // wmax if s >= wmax else 0
    return pl.pallas_call(
        functools.partial(_pool_kernel, has_hist=s < wmax, wmax=wmax),
        grid=(ng, b, s // ts),
        in_specs=[pl.BlockSpec((None, ts, gsz), lambda g, bi, i: (bi, i, g)),
                  pl.BlockSpec((None, wmax, gsz), lambda g, bi, i: (bi, jnp.maximum(i * r16 - 1, 0), g)),
                  pl.BlockSpec((None, wmax, gsz), lambda g, bi, i: (bi, 0, g)),
                  pl.BlockSpec((None, None, gsz, gsz), lambda g, bi, i: (layer, g, 0, 0)),
                  pl.BlockSpec((1, gsz), lambda g, bi, i: (0, g))],
        out_specs=pl.BlockSpec((None, ts, gsz), lambda g, bi, i: (bi, i, g)),
        out_shape=jax.ShapeDtypeStruct((b, s, d), F32),
        scratch_shapes=[pltpu.VMEM((wmax + ts, gsz), F32)],
        compiler_params=_cp("parallel", "parallel", "parallel"),
        name="pool_mix",
    )(u, prev_arr, hist, pool_w, pool_scale.reshape(1, d))


def _first_of_expert(be_ref, blk):
    return (blk == 0) | (be_ref[blk] != be_ref[jnp.maximum(blk - 1, 0)])


def _stream_expert_weights(be_ref, nx_ref, copies, cast):
    j, blk = pl.program_id(0), pl.program_id(1)
    nj = pl.num_programs(0)

    @pl.when((j == 0) & (blk == 0))
    def _():
        for c in copies(j, be_ref[blk]):
            c.start()

    @pl.when(_first_of_expert(be_ref, blk))
    def _():
        for c in copies(j, be_ref[blk]):
            c.wait()
        cast()
        nxt = nx_ref[blk]

        @pl.when(nxt >= 0)
        def _():
            for c in copies(j, nxt):
                c.start()

        @pl.when((nxt < 0) & (j + 1 < nj))
        def _():
            for c in copies(j + 1, be_ref[0]):
                c.start()


def _moe_gu_kernel(be_ref, nx_ref, nu_ref, x_ref, bg_ref, bu_ref, w_ref, h_ref, stg_ref, stu_ref, wgb_ref, wub_ref,
                   sem, *, layer):
    blk = pl.program_id(1)
    tn = stg_ref.shape[1]
    f = w_ref.shape[3] // 2

    def copies(jj, e):
        cg = pl.ds(pl.multiple_of(jj * tn, tn), tn)
        cu = pl.ds(pl.multiple_of(f + jj * tn, tn), tn)
        return (pltpu.make_async_copy(w_ref.at[layer, e, :, cg], stg_ref, sem.at[0]),
                pltpu.make_async_copy(w_ref.at[layer, e, :, cu], stu_ref, sem.at[1]))

    def cast():
        wgb_ref[...] = stg_ref[...].astype(BF16)
        wub_ref[...] = stu_ref[...].astype(BF16)

    _stream_expert_weights(be_ref, nx_ref, copies, cast)

    @pl.when(blk < nu_ref[0])
    def _():
        x = x_ref[...]
        g = jnp.dot(x, wgb_ref[...], preferred_element_type=F32) + bg_ref[...]
        up = jnp.dot(x, wub_ref[...], preferred_element_type=F32) + bu_ref[...]
        g = jnp.minimum(g, SWIGLU_LIMIT)
        up = jnp.clip(up, -SWIGLU_LIMIT, SWIGLU_LIMIT)
        h_ref[...] = (g * jax.nn.sigmoid(SWIGLU_ALPHA * g) * (up + 1.0)).astype(h_ref.dtype)

    @pl.when(blk >= nu_ref[0])
    def _():
        h_ref[...] = jnp.zeros(h_ref.shape, h_ref.dtype)


def _moe_down_kernel(be_ref, nx_ref, nu_ref, h_ref, b_ref, w_ref, o_ref, st_ref, wb_ref, sem, *, layer):
    blk = pl.program_id(1)
    tn = st_ref.shape[1]

    def copies(jj, e):
        return (pltpu.make_async_copy(w_ref.at[layer, e, :, pl.ds(pl.multiple_of(jj * tn, tn), tn)], st_ref, sem.at[0]),)

    def cast():
        wb_ref[...] = st_ref[...].astype(BF16)

    _stream_expert_weights(be_ref, nx_ref, copies, cast)

    @pl.when(blk < nu_ref[0])
    def _():
        o_ref[...] = (jnp.dot(h_ref[...], wb_ref[...], preferred_element_type=F32) + b_ref[...]).astype(o_ref.dtype)

    @pl.when(blk >= nu_ref[0])
    def _():
        o_ref[...] = jnp.zeros(o_ref.shape, o_ref.dtype)


def moe_experts(xs, blk_exp, blk_next, n_used, w_gu, b_gu, w_down, b_down, layer):
    p, d = xs.shape
    ne, _, f2 = w_gu.shape[1:]
    f = f2 // 2
    tm = MOE_TM
    nblk = p // tm
    tn = min(MOE_TN, f)
    tn2 = min(MOE_TN2, d)
    last = lambda blk, nu: jnp.minimum(blk, nu[0] - 1)
    hbm = pl.BlockSpec(memory_space=pl.ANY)
    gs1 = pltpu.PrefetchScalarGridSpec(
        num_scalar_prefetch=3, grid=(f // tn, nblk),
        in_specs=[pl.BlockSpec((tm, d), lambda j, i, be, nx, nu: (last(i, nu), 0)),
                  pl.BlockSpec((None, None, 1, tn), lambda j, i, be, nx, nu: (layer, be[i], 0, j)),
                  pl.BlockSpec((None, None, 1, tn), lambda j, i, be, nx, nu: (layer, be[i], 0, f // tn + j)),
                  hbm],
        out_specs=pl.BlockSpec((tm, tn), lambda j, i, be, nx, nu: (i, j)),
        scratch_shapes=[pltpu.VMEM((d, tn), F32), pltpu.VMEM((d, tn), F32),
                        pltpu.VMEM((d, tn), BF16), pltpu.VMEM((d, tn), BF16), pltpu.SemaphoreType.DMA((2,))])
    b_gu4 = b_gu.reshape(b_gu.shape[0], ne, 1, f2)
    h = pl.pallas_call(
        functools.partial(_moe_gu_kernel, layer=layer), grid_spec=gs1,
        out_shape=jax.ShapeDtypeStruct((p, f), BF16),
        compiler_params=_cp("arbitrary", "arbitrary"),
        name="moe_gate_up",
    )(blk_exp, blk_next, n_used, xs, b_gu4, b_gu4, w_gu)
    gs2 = pltpu.PrefetchScalarGridSpec(
        num_scalar_prefetch=3, grid=(d // tn2, nblk),
        in_specs=[pl.BlockSpec((tm, f), lambda j, i, be, nx, nu: (last(i, nu), 0)),
                  pl.BlockSpec((None, None, 1, tn2), lambda j, i, be, nx, nu: (layer, be[i], 0, j)),
                  hbm],
        out_specs=pl.BlockSpec((tm, tn2), lambda j, i, be, nx, nu: (i, j)),
        scratch_shapes=[pltpu.VMEM((f, tn2), F32), pltpu.VMEM((f, tn2), BF16), pltpu.SemaphoreType.DMA((1,))])
    return pl.pallas_call(
        functools.partial(_moe_down_kernel, layer=layer), grid_spec=gs2,
        out_shape=jax.ShapeDtypeStruct((p, d), BF16),
        compiler_params=_cp("arbitrary", "arbitrary"),
        name="moe_down",
    )(blk_exp, blk_next, n_used, h, b_down.reshape(b_down.shape[0], ne, 1, d), w_down)


def moe_route(logits, n_exp):
    t = logits.shape[0]
    tm = MOE_TM
    a = t * TOP_K
    top_v, top_i = lax.top_k(logits, TOP_K)
    gates = jax.nn.softmax(top_v, axis=-1)
    experts = jnp.arange(n_exp, dtype=jnp.int32)
    hit = top_i[:, :, None] == experts
    chosen = jnp.sum(hit, axis=1).astype(F32)
    counts = jnp.sum(chosen, axis=0).astype(jnp.int32)
    padded = (counts + tm - 1) // tm * tm
    pad_end = jnp.cumsum(padded)
    pad_start = pad_end - padded
    start = jnp.cumsum(counts) - counts
    ch = 256
    nch = -(-t // ch)
    cp = jnp.pad(chosen, ((0, nch * ch - t), (0, 0))).reshape(nch, ch, n_exp)
    tri = (jnp.arange(ch)[:, None] > jnp.arange(ch)[None, :]).astype(F32)
    within = jnp.einsum('ij,cje->cie', tri, cp)
    tot = jnp.sum(cp, axis=1)
    before = (within + (jnp.cumsum(tot, axis=0) - tot)[:, None, :]).reshape(nch * ch, n_exp)[:t]
    rank = jnp.sum(jnp.where(hit, before[:, None, :], 0.0), axis=-1).astype(jnp.int32)
    pos = jnp.sum(jnp.where(hit, pad_start, 0), axis=-1).astype(jnp.int32) + rank
    flat_e = top_i.reshape(-1).astype(jnp.int32)
    keys = jnp.sort(flat_e * a + jnp.arange(a, dtype=jnp.int32))
    order = keys - (keys // a) * a
    nblk = -(-a // tm) + n_exp
    blk_first = jnp.arange(nblk, dtype=jnp.int32) * tm
    blk_exp = jnp.minimum(jnp.sum(pad_end[None, :] <= blk_first[:, None], axis=1), n_exp - 1).astype(jnp.int32)
    n_used = (pad_end[-1] // tm).astype(jnp.int32)
    r0 = blk_first - pad_start[blk_exp]
    r = r0[:, None] + jnp.arange(tm, dtype=jnp.int32)[None, :]
    valid = (r < counts[blk_exp][:, None]) & (jnp.arange(nblk)[:, None] < n_used)
    src = jnp.clip(start[blk_exp][:, None] + r, 0, a - 1)
    slot_tok = jnp.where(valid, jnp.take(order, src, mode="clip") // TOP_K, t).reshape(-1).astype(jnp.int32)
    blk_exp = jnp.where(jnp.arange(nblk) < n_used, blk_exp, blk_exp[n_used - 1]).astype(jnp.int32)
    later = jnp.where((experts[None, :] > experts[:, None]) & (counts[None, :] > 0), experts[None, :], n_exp)
    nxt = jnp.min(later, axis=1)
    blk_next = jnp.where(nxt < n_exp, nxt, -1)[blk_exp].astype(jnp.int32)
    return gates, slot_tok, pos, blk_exp, blk_next, n_used.reshape(1)


def _pad_seq(t, n):
    return t if t.shape[1] == n else jnp.pad(t, ((0, 0), (0, n - t.shape[1]), (0, 0)))


def _mlstm_layer(u, gates_pre, w_in, norm_g, w_out, j, carry):
    b, s, d = u.shape
    c0, n0, m0 = carry
    h_, dk, dv = c0.shape[1:]
    L = ML_L if s % ML_L == 0 else -(-s // ML_L_MIN) * ML_L_MIN
    sp = -(-s // L) * L
    if sp != s:
        u = _pad_seq(u, sp)
        padg = jnp.concatenate([jnp.full((b, sp - s, h_), NEG, F32), jnp.full((b, sp - s, h_), -NEG, F32)], -1)
        gates_pre = jnp.concatenate([gates_pre, padg], 1)
    nqkv = 2 * h_ * dk + h_ * dv
    u2 = u.reshape(b * sp, d)
    proj = matmul(u2, w_in, j, 0, nqkv, BF16)
    ogate = matmul(u2, w_in, j, nqkv, h_ * dv, F32)
    hg, c_f, n_f, m_f = mlstm_core(proj, ogate, gates_pre, norm_g, c0, n0, m0, b, sp, L)
    y = matmul(hg, w_out, j, 0, d, F32).reshape(b, sp, d)
    return y[:, :s], (c_f, n_f, m_f)


def _moba_layer(u, w_qkv, w_out, j, heads, past):
    b, s, d = u.shape
    hd = d // heads
    x2 = u.reshape(b * s, d)
    q, k, v = (matmul(x2, w_qkv, j, c * d, d, F32).reshape(b, s, d) for c in range(3))
    p0 = 0 if past is None else past[2].shape[1] * past[0].shape[2]
    tabs = rope_tables(p0 + jnp.arange(s), hd)
    q, k = rope(q, tabs, hd), rope(k, tabs, hd)
    if past is None:
        o = moba_prompt(q.reshape(b * s, d), k.reshape(b * s, d), v.reshape(b * s, d), b, s, heads)
    else:
        cache_k, cache_v, page_table = past
        o = moba_sample(q, k, v, cache_k, cache_v, j, page_table, heads).reshape(b * s, d)
    y = matmul(o, w_out, j, 0, d, F32).reshape(b, s, d)
    return y, (k.reshape(b, s, heads, hd), v.reshape(b, s, heads, hd))


def kernel(x_prompt, x_sample, c_prompt, c_sample, state_mlstm_C, state_mlstm_n, state_mlstm_m, cache_moba_k, cache_moba_v, page_table, state_pool, ada_w, ada_b, ln_g, ln_b, mlstm_w_in, mlstm_b_gates, mlstm_norm_g, mlstm_w_out, moba_w_qkv, moba_w_out, pool_w, pool_scale, router_w, router_b, moe_w_gu, moe_b_gu, moe_w_down, moe_b_down):
    depth, d = ada_w.shape[0], ada_w.shape[1]
    bp, sp_, _ = x_prompt.shape
    bs, ss, _ = x_sample.shape
    n_exp = router_w.shape[2]
    ml_heads, ml_dk, ml_dv = state_mlstm_C.shape[2:]
    moba_heads = cache_moba_k.shape[3]
    alpha = (2.0 * depth) ** 0.25
    nproj = 2 * ml_heads * (ml_dk + ml_dv)

    nreq = bp + bs
    rpad = -(-nreq // 8) * 8
    c_all = jnp.pad(jnp.concatenate([c_prompt, c_sample], 0), ((0, rpad - nreq), (0, 0)))
    mods_all = ada_mod(c_all, ada_w, ada_b)
    groups = [dict(x=x_prompt, lo=0, b=bp), dict(x=x_sample, lo=bp, b=bs)]

    def mods_of(i, grp):
        return mods_all[i, grp['lo']:grp['lo'] + grp['b']].reshape(grp['b'], 1, 6 * d)

    def mixer_aux(i):
        if i % N_MIXERS == 0:
            jj = i // N_MIXERS
            return mlstm_w_in[jj][:, nproj:], mlstm_b_gates[jj]
        return None, None

    def u_dtype(i):
        return F32 if i % N_MIXERS == 2 else BF16

    aw, ab = mixer_aux(0)
    for grp in groups:
        outs = ln_mod(grp['x'], mods_of(0, grp), mod_c=(0, 1), aux_w=aw, aux_b=ab, u_dtype=u_dtype(0))
        grp['u'] = outs[0]
        grp['aux'] = outs[1] if aw is not None else None

    ml, mb, plst = ([], []), ([], []), ([], [])
    for i in range(depth):
        kind, j = i % N_MIXERS, i // N_MIXERS
        ys = []
        for gi, grp in enumerate(groups):
            u = grp['u']
            b, s, _ = u.shape
            if kind == 0:
                if gi == 0:
                    carry = (jnp.zeros((b, ml_heads, ml_dk, ml_dv), F32), jnp.zeros((b, ml_heads, ml_dk), F32),
                             jnp.full((b, ml_heads), ML_M_INIT, F32))
                else:
                    carry = (state_mlstm_C[j].astype(F32), state_mlstm_n[j].astype(F32), state_mlstm_m[j].astype(F32))
                y, st = _mlstm_layer(u, grp['aux'], mlstm_w_in, mlstm_norm_g[j], mlstm_w_out, j, carry)
                ml[gi].append(st)
            elif kind == 1:
                past = None if gi == 0 else (cache_moba_k, cache_moba_v, page_table)
                y, st = _moba_layer(u, moba_w_qkv, moba_w_out, j, moba_heads, past)
                mb[gi].append(st)
            else:
                wmax = max(POOL_WINDOWS)
                if gi == 0:
                    hist = jnp.zeros((b, wmax, d), F32)
                    ext = u
                else:
                    hist = jnp.pad(state_pool[j].astype(F32), ((0, 0), (1, 0), (0, 0)))
                    ext = jnp.concatenate([state_pool[j].astype(F32), u], 1)
                y = pool_mix(u, hist, pool_w, j, pool_scale[j])
                plst[gi].append(ext[:, ext.shape[1] - (wmax - 1):])
            ys.append(y)

        for grp, y in zip(groups, ys):
            grp['x'], grp['u'], grp['aux'] = ln_mod(
                grp['x'], mods_of(i, grp), y=y, gate_c=2, ln_g=ln_g[i, 0], ln_b=ln_b[i, 0], mod_c=(3, 4),
                aux_w=router_w[i], aux_b=router_b[i], u_dtype=BF16, alpha=alpha)

        toks = [g_['u'].reshape(-1, d) for g_ in groups]
        logits = jnp.concatenate([g_['aux'].reshape(-1, n_exp) for g_ in groups], 0)
        t_all = logits.shape[0]
        gates, slot_tok, pos, blk_exp, blk_next, n_used = moe_route(logits, n_exp)
        x_pad = jnp.concatenate(toks + [jnp.zeros((1, d), BF16)], 0)
        xs = jnp.take(x_pad, slot_tok, axis=0, mode="clip")
        ys_slots = moe_experts(xs, blk_exp, blk_next, n_used, moe_w_gu, moe_b_gu, moe_w_down, moe_b_down, i)

        last = i == depth - 1
        aw, ab = (None, None) if last else mixer_aux(i + 1)
        t0 = 0
        for grp in groups:
            b, s, _ = grp['x'].shape
            n = b * s
            pg = pos[t0:t0 + n]
            ysg = jnp.take(ys_slots, pg.T.reshape(-1), axis=0, mode="clip").reshape(TOP_K, b, s, d)
            gt = gates[t0:t0 + n].reshape(b, s, TOP_K)
            t0 += n
            outs = ln_mod(grp['x'], mods_of(i, grp) if last else jnp.concatenate([mods_of(i, grp), mods_of(i + 1, grp)], -1),
                          ys=ysg, gates=gt, gate_c=5, ln_g=ln_g[i, 1], ln_b=ln_b[i, 1],
                          mod_c=None if last else (6, 7), aux_w=aw, aux_b=ab,
                          u_dtype=BF16 if last else u_dtype(i + 1), alpha=alpha)
            grp['x'] = outs[0]
            if not last:
                grp['u'] = outs[1]
                grp['aux'] = outs[2] if aw is not None else None

    xdt = x_prompt.dtype
    sdt = state_mlstm_C.dtype
    stack = lambda sts, k, dt: jnp.stack([s_[k] for s_ in sts]).astype(dt)
    return (groups[0]['x'], groups[1]['x'],
            stack(ml[0], 0, xdt), stack(ml[0], 1, xdt), stack(ml[0], 2, xdt),
            stack(ml[1], 0, sdt), stack(ml[1], 1, sdt), stack(ml[1], 2, sdt),
            stack(mb[0], 0, xdt), stack(mb[0], 1, xdt), stack(mb[1], 0, xdt), stack(mb[1], 1, xdt),
            jnp.stack(plst[0]), jnp.stack(plst[1]))
```

```python
import functools

import jax
import jax.numpy as jnp
from jax import lax
from jax.experimental import pallas as pl
from jax.experimental.pallas import tpu as pltpu

F32 = jnp.float32
BF16 = jnp.bfloat16

N_MIXERS = 3
ML_M_INIT = -1.0e30
ML_CHUNK = 128
MOBA_BLOCK = 256
MOBA_TOPK = 3
ROPE_THETA = 500000.0
POOL_WINDOWS = (2, 4, 8, 16)
TOP_K = 4
SWIGLU_LIMIT = 7.0
SWIGLU_ALPHA = 1.702
LN_EPS = 1e-5
RMS_EPS = 1e-6

LANES = 128
VMEM_LIMIT = 56 * 1024 * 1024
MM_TM = 1024
MM_TN = 512
LN_TS = 128
MOE_TM = 256
MOE_TN = 512
MOE_TN2 = 2048
ML_L = 256
ML_L_MIN = 16
POOL_TS = 256
NEG = -1.0e30


def _cp(*sem):
    return pltpu.CompilerParams(dimension_semantics=sem, vmem_limit_bytes=VMEM_LIMIT)


def _ada_kernel(c_ref, w_ref, b_ref, o_ref):
    c = c_ref[...]
    a = (c * jax.nn.sigmoid(c)).astype(BF16)
    o_ref[...] = jnp.dot(a, w_ref[...].astype(BF16), preferred_element_type=F32) + b_ref[...]


def ada_mod(c_all, ada_w, ada_b):
    depth, d, n = ada_w.shape
    r = c_all.shape[0]
    tn = min(MM_TN, n)
    return pl.pallas_call(
        _ada_kernel,
        grid=(depth, n // tn),
        in_specs=[pl.BlockSpec((r, d), lambda l, j: (0, 0)),
                  pl.BlockSpec((None, d, tn), lambda l, j: (l, 0, j)),
                  pl.BlockSpec((None, 1, tn), lambda l, j: (l, 0, j))],
        out_specs=pl.BlockSpec((None, r, tn), lambda l, j: (l, 0, j)),
        out_shape=jax.ShapeDtypeStruct((depth, r, n), F32),
        compiler_params=_cp("arbitrary", "arbitrary"),
        name="ada_mod",
    )(c_all, ada_w, ada_b.reshape(depth, 1, n))


def _ln_mod_kernel(*refs, res, has_ln, has_mod, has_aux, alpha):
    it = iter(refs)
    x_ref = next(it)
    x = x_ref[...]
    if has_ln:
        if res == "dense":
            y = next(it)[...].astype(F32)
        else:
            ys_ref = next(it)
            gt = next(it)[...]
            y = gt[:, 0:1] * ys_ref[0].astype(F32)
            for k in range(1, ys_ref.shape[0]):
                y = y + gt[:, k:k + 1] * ys_ref[k].astype(F32)
        gate = next(it)[...]
        lg = next(it)[...]
        lb = next(it)[...]
        z = alpha * x + gate * y
        mu = jnp.mean(z, axis=-1, keepdims=True)
        zc = z - mu
        var = jnp.mean(zc * zc, axis=-1, keepdims=True)
        x = zc * lax.rsqrt(var + LN_EPS) * lg + lb
    if has_mod:
        sc = next(it)[...]
        sh = next(it)[...]
        u = x * (1.0 + sc) + sh
    if has_aux:
        wa = next(it)[...]
        ba = next(it)[...]
    if has_ln:
        next(it)[...] = x
    if has_mod:
        u_ref = next(it)
        u_ref[...] = u.astype(u_ref.dtype)
    if has_aux:
        next(it)[...] = jnp.dot(u.astype(BF16), wa.astype(BF16), preferred_element_type=F32) + ba


def ln_mod(x, mods, *, y=None, ys=None, gates=None, gate_c=None, ln_g=None, ln_b=None,
           mod_c=None, aux_w=None, aux_b=None, u_dtype=BF16, alpha=1.0):
    b, s, d = x.shape
    ts = min(LN_TS, s)
    grid = (b, s // ts)
    row = pl.BlockSpec((None, ts, d), lambda i, j: (i, j, 0))

    def modspec(c):
        return pl.BlockSpec((None, 1, d), lambda i, j, c=c: (i, 0, c))

    vec = pl.BlockSpec((1, d), lambda i, j: (0, 0))
    has_ln = gate_c is not None
    has_mod = mod_c is not None
    has_aux = aux_w is not None
    res = "dense" if y is not None else "moe"
    args, specs = [x], [row]
    if has_ln:
        if y is not None:
            args.append(y)
            specs.append(row)
        else:
            k = ys.shape[0]
            args += [ys, gates]
            specs += [pl.BlockSpec((k, None, ts, d), lambda i, j: (0, i, j, 0)),
                      pl.BlockSpec((None, ts, k), lambda i, j: (i, j, 0))]
        args += [mods, ln_g.reshape(1, d), ln_b.reshape(1, d)]
        specs += [modspec(gate_c), vec, vec]
    if has_mod:
        args += [mods, mods]
        specs += [modspec(mod_c[1]), modspec(mod_c[0])]
    if has_aux:
        na = aux_w.shape[1]
        args += [aux_w, aux_b.reshape(1, na)]
        specs += [pl.BlockSpec((d, na), lambda i, j: (0, 0)), pl.BlockSpec((1, na), lambda i, j: (0, 0))]
    out_shape, out_specs = [], []
    if has_ln:
        out_shape.append(jax.ShapeDtypeStruct((b, s, d), F32))
        out_specs.append(row)
    if has_mod:
        out_shape.append(jax.ShapeDtypeStruct((b, s, d), u_dtype))
        out_specs.append(row)
    if has_aux:
        out_shape.append(jax.ShapeDtypeStruct((b, s, na), F32))
        out_specs.append(pl.BlockSpec((None, ts, na), lambda i, j: (i, j, 0)))
    return pl.pallas_call(
        functools.partial(_ln_mod_kernel, res=res, has_ln=has_ln, has_mod=has_mod, has_aux=has_aux, alpha=alpha),
        grid=grid, in_specs=specs, out_specs=out_specs, out_shape=out_shape,
        compiler_params=_cp("parallel", "parallel"),
        name="ln_mod",
    )(*args)


def _mm_kernel(x_ref, w_ref, o_ref, wb_ref):
    @pl.when(pl.program_id(1) == 0)
    def _():
        wb_ref[...] = w_ref[...].astype(BF16)

    o_ref[...] = jnp.dot(x_ref[...], wb_ref[...], preferred_element_type=F32).astype(o_ref.dtype)


def matmul(x, w, layer, col0, n, out_dtype):
    m, k = x.shape
    tm = min(MM_TM, m)
    tn = min(MM_TN, n)
    assert m % tm == 0 and n % tn == 0 and col0 % tn == 0
    c0 = col0 // tn
    return pl.pallas_call(
        _mm_kernel,
        grid=(n // tn, m // tm),
        in_specs=[pl.BlockSpec((tm, k), lambda j, i: (i, 0)),
                  pl.BlockSpec((None, k, tn), lambda j, i: (layer, 0, c0 + j))],
        out_specs=pl.BlockSpec((tm, tn), lambda j, i: (i, j)),
        out_shape=jax.ShapeDtypeStruct((m, n), out_dtype),
        scratch_shapes=[pltpu.VMEM((k, tn), BF16)],
        compiler_params=_cp("arbitrary", "arbitrary"),
        name="matmul",
    )(x, w)


def _split3(x):
    hi = x.astype(BF16)
    r1 = x - hi.astype(F32)
    mid = r1.astype(BF16)
    lo = (r1 - mid.astype(F32)).astype(BF16)
    return hi, mid, lo


def _log_sigmoid(x):
    return jnp.minimum(x, 0.0) - jnp.log(1.0 + jnp.exp(-jnp.abs(x)))


def _mlstm_kernel(q_ref, k_ref, v_ref, o_ref, gr_ref, gc_ref, ng_ref, c0_ref, n0_ref, m0_ref,
                  h_ref, co_ref, no_ref, mo_ref, c_sc, n_sc, m_sc, *, scale):
    c_idx = pl.program_id(2)
    nc = pl.num_programs(2)
    L = q_ref.shape[0]

    @pl.when(c_idx == 0)
    def _():
        c_sc[...] = c0_ref[...]
        n_sc[...] = n0_ref[...]
        m_sc[...] = m0_ref[...]

    q = q_ref[...]
    k = k_ref[...]
    v = v_ref[...]
    gr = gr_ref[...]
    gc = gc_ref[...]
    logi_r = gr[0:1, :]
    logf_r = _log_sigmoid(gr[1:2, :])
    logi_c = gc[:, 0:1]
    logf_c = _log_sigmoid(gc[:, 1:2])
    ri = lax.broadcasted_iota(jnp.int32, (L, L), 0)
    ci = lax.broadcasted_iota(jnp.int32, (L, L), 1)
    causal = ci <= ri
    upper = jnp.where(ri <= ci, 1.0, 0.0).astype(BF16)
    lower = jnp.where(ci <= ri, 1.0, 0.0).astype(BF16)
    fr = jnp.broadcast_to(logf_r, (8, L))
    fc = jnp.broadcast_to(logf_c, (L, LANES))
    b_r = sum(jnp.dot(p, upper, preferred_element_type=F32) for p in _split3(fr))[0:1, :]
    b_c = sum(jnp.dot(lower, p, preferred_element_type=F32) for p in _split3(fc))[:, 0:1]
    m_prev = m_sc[...]
    logw = jnp.where(causal, b_c - b_r + logi_r, -jnp.inf)
    m_inter = m_prev + b_c
    m_t = jnp.maximum(m_inter, jnp.max(logw, axis=-1, keepdims=True))
    s = lax.dot_general(q, k, (((1,), (1,)), ((), ())), preferred_element_type=F32)
    s = s * scale * jnp.exp(logw - m_t)
    inter = jnp.exp(m_inter - m_t)
    c_mat = c_sc[...]
    n_vec = n_sc[...]
    qf = q.astype(F32) * scale
    num = jnp.dot(s.astype(BF16), v, preferred_element_type=F32) + inter * (
        jnp.dot(q, c_mat.astype(BF16), preferred_element_type=F32) * scale)
    den = jnp.sum(s, axis=-1, keepdims=True) + inter * jnp.sum(
        qf * n_vec.astype(BF16).astype(F32), axis=-1, keepdims=True)
    h = num / jnp.maximum(jnp.abs(den), jnp.exp(-m_t))
    m_new = m_t[L - 1:L, :]
    b_last = b_c[L - 1:L, :]
    w_end = jnp.exp(b_last - b_c + logi_c - m_new)
    decay = jnp.exp(m_prev + b_last - m_new)
    kw = k.astype(F32) * w_end
    c_new = decay * c_mat + lax.dot_general(kw.astype(BF16), v, (((0,), (0,)), ((), ())),
                                            preferred_element_type=F32)
    n_new = decay * n_vec + jnp.sum(kw, axis=0, keepdims=True)
    c_sc[...] = c_new
    n_sc[...] = n_new
    m_sc[...] = m_new
    hn = h * lax.rsqrt(jnp.mean(h * h, axis=-1, keepdims=True) + RMS_EPS) * ng_ref[...]
    h_ref[...] = (jax.nn.sigmoid(o_ref[...].astype(F32)) * hn).astype(h_ref.dtype)

    @pl.when(c_idx == nc - 1)
    def _():
        co_ref[...] = c_new
        no_ref[...] = n_new
        mo_ref[...] = m_new


def mlstm_core(proj, ogate, gates, norm_g, c0, n0, m0, bsz, seq, L):
    h_, dk, dv = c0.shape[1:]
    nc = seq // L
    g = gates.reshape(bsz, nc, L, 2, h_)
    g_row = g.transpose(0, 4, 1, 3, 2)
    g_col = g.transpose(0, 4, 1, 2, 3)
    kq, kk, kv = 0, h_ * dk // dk, 2 * h_ * dk // dv
    row = lambda b, h, c: b * nc + c
    out = pl.pallas_call(
        functools.partial(_mlstm_kernel, scale=float(dk) ** -0.5),
        grid=(bsz, h_, nc),
        in_specs=[pl.BlockSpec((L, dk), lambda b, h, c: (row(b, h, c), kq + h)),
                  pl.BlockSpec((L, dk), lambda b, h, c: (row(b, h, c), kk + h)),
                  pl.BlockSpec((L, dv), lambda b, h, c: (row(b, h, c), kv + h)),
                  pl.BlockSpec((L, dv), lambda b, h, c: (row(b, h, c), h)),
                  pl.BlockSpec((None, None, None, 2, L), lambda b, h, c: (b, h, c, 0, 0)),
                  pl.BlockSpec((None, None, None, L, 2), lambda b, h, c: (b, h, c, 0, 0)),
                  pl.BlockSpec((1, dv), lambda b, h, c: (0, h)),
                  pl.BlockSpec((None, None, dk, dv), lambda b, h, c: (b, h, 0, 0)),
                  pl.BlockSpec((None, None, 1, dk), lambda b, h, c: (b, h, 0, 0)),
                  pl.BlockSpec((None, None, 1, 1), lambda b, h, c: (b, h, 0, 0))],
        out_specs=[pl.BlockSpec((L, dv), lambda b, h, c: (row(b, h, c), h)),
                   pl.BlockSpec((None, None, dk, dv), lambda b, h, c: (b, h, 0, 0)),
                   pl.BlockSpec((None, None, 1, dk), lambda b, h, c: (b, h, 0, 0)),
                   pl.BlockSpec((None, None, 1, 1), lambda b, h, c: (b, h, 0, 0))],
        out_shape=[jax.ShapeDtypeStruct((bsz * seq, h_ * dv), BF16),
                   jax.ShapeDtypeStruct((bsz, h_, dk, dv), F32),
                   jax.ShapeDtypeStruct((bsz, h_, 1, dk), F32),
                   jax.ShapeDtypeStruct((bsz, h_, 1, 1), F32)],
        scratch_shapes=[pltpu.VMEM((dk, dv), F32), pltpu.VMEM((1, dk), F32), pltpu.VMEM((1, 1), F32)],
        compiler_params=_cp("parallel", "parallel", "arbitrary"),
        name="mlstm",
    )(proj, proj, proj, ogate, g_row, g_col, norm_g.reshape(1, h_ * dv),
      c0, n0.reshape(bsz, h_, 1, dk), m0.reshape(bsz, h_, 1, 1))
    hg, c_f, n_f, m_f = out
    return hg, c_f, n_f.reshape(bsz, h_, dk), m_f.reshape(bsz, h_)


def rope_tables(pos, hd):
    rot = hd // 4
    half = rot // 2
    inv = ROPE_THETA ** (-jnp.arange(half, dtype=F32) * 2.0 / rot)
    ang = pos.astype(F32)[:, None] * inv[None, :]
    cos, sin = jnp.cos(ang), jnp.sin(ang)
    n = pos.shape[0]
    rest = hd - rot
    t_cos = jnp.concatenate([cos, cos, jnp.ones((n, rest), F32)], -1)
    t_up = jnp.concatenate([-sin, jnp.zeros((n, hd - half), F32)], -1)
    t_dn = jnp.concatenate([jnp.zeros((n, half), F32), sin, jnp.zeros((n, rest), F32)], -1)
    return t_cos, t_up, t_dn


def _rope_kernel(x_ref, c_ref, a_ref, b_ref, o_ref, *, hd, half):
    w = x_ref.shape[-1]
    x = x_ref[...]
    up = pltpu.roll(x, w - half, 1)
    dn = pltpu.roll(x, half, 1)
    c, a, b = c_ref[...], a_ref[...], b_ref[...]
    for h in range(w // hd):
        sl = slice(h * hd, (h + 1) * hd)
        o_ref[:, sl] = x[:, sl] * c + up[:, sl] * a + dn[:, sl] * b


def rope(x, tabs, hd):
    b, s, d = x.shape
    ts = min(256, s)
    w = min(d, 8 * hd)
    tab = pl.BlockSpec((ts, hd), lambda i, j, c: (j, 0))
    blk = pl.BlockSpec((None, ts, w), lambda i, j, c: (i, j, c))
    return pl.pallas_call(
        functools.partial(_rope_kernel, hd=hd, half=hd // 8),
        grid=(b, s // ts, d // w),
        in_specs=[blk, tab, tab, tab],
        out_specs=blk,
        out_shape=jax.ShapeDtypeStruct((b, s, d), F32),
        compiler_params=_cp("parallel", "parallel", "parallel"),
        name="rope",
    )(x, *tabs)


def _top_rows(gate, n_sel):
    row = lax.broadcasted_iota(jnp.int32, gate.shape, 0).astype(F32)
    sel = jnp.zeros(gate.shape, F32)
    g = gate
    for _ in range(n_sel):
        mx = jnp.max(g, axis=0, keepdims=True)
        first = jnp.min(jnp.where(g == mx, row, float(gate.shape[0])), axis=0, keepdims=True)
        hit = (row == first) & (mx > -jnp.inf)
        sel = jnp.where(hit, 1.0, sel)
        g = jnp.where(hit, -jnp.inf, g)
    return sel


def _moba_prompt_kernel(q_ref, k_ref, v_ref, o_ref, km_ref, kb_ref, vt_ref, *, blk, n_sel, scale, uses):
    qi = pl.program_id(2)
    seq = kb_ref.shape[0]
    nb = seq // blk

    @pl.when(qi == 0)
    def _():
        km_ref[...] = jnp.zeros(km_ref.shape, F32)
        for n in range(nb):
            kblk = k_ref[n * blk:(n + 1) * blk, :]
            km_ref[n:n + 1, :] = jnp.mean(kblk, axis=0, keepdims=True)
            kb_ref[n * blk:(n + 1) * blk, :] = kblk.astype(BF16)
            vt_ref[:, n * blk:(n + 1) * blk] = v_ref[n * blk:(n + 1) * blk, :].T.astype(BF16)

    qtb = q_ref[...].T.astype(BF16)
    gate = jnp.dot(km_ref[...].astype(BF16), qtb, preferred_element_type=F32)
    row = lax.broadcasted_iota(jnp.int32, gate.shape, 0)
    sel = _top_rows(jnp.where(row < qi, gate, -jnp.inf), n_sel)
    start = pl.multiple_of(qi * blk, blk)
    ki = lax.broadcasted_iota(jnp.int32, (blk, blk), 0)
    ci = lax.broadcasted_iota(jnp.int32, (blk, blk), 1)
    s_own = jnp.dot(kb_ref[pl.ds(start, blk), :], qtb, preferred_element_type=F32) * scale
    s_own = jnp.where(ki <= ci, s_own, NEG)

    def attend(nuse):
        s_all = jnp.dot(kb_ref[0:nuse * blk, :], qtb, preferred_element_type=F32) * scale
        s_blk = [jnp.where(sel[n:n + 1, :] > 0.0, s_all[n * blk:(n + 1) * blk, :], NEG) for n in range(nuse)]
        m = jnp.max(s_own, axis=0, keepdims=True)
        for sb in s_blk:
            m = jnp.maximum(m, jnp.max(sb, axis=0, keepdims=True))
        p_own = jnp.exp(s_own - m)
        l = jnp.sum(p_own, axis=0, keepdims=True)
        p_blk = [jnp.exp(sb - m) for sb in s_blk]
        for pb in p_blk:
            l = l + jnp.sum(pb, axis=0, keepdims=True)
        p_all = jnp.concatenate([pb.astype(BF16) for pb in p_blk], axis=0)
        acc = jnp.dot(vt_ref[:, pl.ds(start, blk)], p_own.astype(BF16), preferred_element_type=F32)
        acc = acc + jnp.dot(vt_ref[:, 0:nuse * blk], p_all, preferred_element_type=F32)
        o_ref[...] = (acc / l).T.astype(o_ref.dtype)

    lo = -1
    for nuse in uses:
        @pl.when((qi > lo) & (qi <= nuse))
        def _(nuse=nuse):
            attend(nuse)
        lo = nuse


def moba_prompt(q, k, v, bsz, seq, heads):
    hd = q.shape[1] // heads
    blk = MOBA_BLOCK
    assert seq % blk == 0
    nb = seq // blk
    nbp = -(-nb // 8) * 8
    return pl.pallas_call(
        functools.partial(_moba_prompt_kernel, blk=blk, n_sel=min(MOBA_TOPK, nb - 1), scale=float(hd) ** -0.5,
                          uses=tuple(sorted({max(1, -(-nb * f // 4)) for f in (1, 2, 3)} | {nb}))),
        grid=(bsz, heads, nb),
        in_specs=[pl.BlockSpec((blk, hd), lambda b, h, i: (b * nb + i, h)),
                  pl.BlockSpec((seq, hd), lambda b, h, i: (b, h)),
                  pl.BlockSpec((seq, hd), lambda b, h, i: (b, h))],
        out_specs=pl.BlockSpec((blk, hd), lambda b, h, i: (b * nb + i, h)),
        out_shape=jax.ShapeDtypeStruct((bsz * seq, heads * hd), BF16),
        scratch_shapes=[pltpu.VMEM((nbp, hd), F32), pltpu.VMEM((seq, hd), BF16), pltpu.VMEM((hd, seq), BF16)],
        compiler_params=_cp("parallel", "parallel", "arbitrary"),
        name="moba_prompt",
    )(q, k, v)


def _page_sum_kernel(pt_ref, *refs):
    o_ref = refs[-1]
    acc = jnp.sum(refs[0][...], axis=0)
    for k_ref in refs[1:-1]:
        acc = acc + jnp.sum(k_ref[...], axis=0)
    o_ref[...] = acc


def block_sums(cache, layer, page_table, ppb):
    _, _, page, heads, hd = cache.shape
    nreq, npg = page_table.shape
    nblk = npg // ppb

    def pg_spec(i):
        return pl.BlockSpec((None, None, page, heads, hd),
                            lambda b, n, pt, i=i: (layer, pt[b * npg + n * ppb + i], 0, 0, 0))

    grid_spec = pltpu.PrefetchScalarGridSpec(
        num_scalar_prefetch=1, grid=(nreq, nblk),
        in_specs=[pg_spec(i) for i in range(ppb)],
        out_specs=pl.BlockSpec((None, None, heads, hd), lambda b, n, pt: (b, n, 0, 0)))
    return pl.pallas_call(
        _page_sum_kernel, grid_spec=grid_spec,
        out_shape=jax.ShapeDtypeStruct((nreq, nblk, heads, hd), F32),
        compiler_params=_cp("parallel", "parallel"),
        name="block_sums",
    )(page_table.reshape(-1), *([cache] * ppb))


def _moba_sample_kernel(pid_ref, ok_ref, q_ref, kn_ref, vn_ref, ck_ref, cv_ref, o_ref, kbuf, vbuf, sem,
                        *, layer, n_sel, ppb, scale):
    nh = pl.num_programs(1)
    step = pl.program_id(0) * nh + pl.program_id(1)
    nsteps = pl.num_programs(0) * nh
    n_q = q_ref.shape[0]
    n_pg = n_sel * ppb
    per = n_q * n_pg

    def copies(st, slot):
        hh = st % nh
        out = []
        for j in range(per):
            pg = pid_ref[st * per + j]
            out.append(pltpu.make_async_copy(ck_ref.at[layer, pg, :, hh, :], kbuf.at[slot, j], sem.at[0, slot]))
            out.append(pltpu.make_async_copy(cv_ref.at[layer, pg, :, hh, :], vbuf.at[slot, j], sem.at[1, slot]))
        return out

    @pl.when(step == 0)
    def _():
        for c in copies(step, 0):
            c.start()

    @pl.when(step + 1 < nsteps)
    def _():
        for c in copies(step + 1, (step + 1) % 2):
            c.start()

    slot = step % 2
    for c in copies(step, slot):
        c.wait()

    rows = lax.broadcasted_iota(jnp.int32, (kn_ref.shape[0], 1), 0)
    rnd = lambda t: t.astype(BF16).astype(F32)
    kn = rnd(kn_ref[...])
    vn = rnd(vn_ref[...])
    for qq in range(n_q):
        q = rnd(q_ref[qq:qq + 1, :])
        s_new = jnp.sum(kn * q, axis=-1, keepdims=True) * scale
        s_new = jnp.where(rows <= qq, s_new, NEG)
        s_pg = []
        for i in range(n_pg):
            s = jnp.sum(rnd(kbuf[slot, qq * n_pg + i]) * q, axis=-1, keepdims=True) * scale
            s_pg.append(jnp.where(ok_ref[(step * n_q + qq) * n_sel + i // ppb] > 0, s, NEG))
        m = jnp.max(s_new, axis=0, keepdims=True)
        for s in s_pg:
            m = jnp.maximum(m, jnp.max(s, axis=0, keepdims=True))
        p = jnp.exp(s_new - m)
        l = jnp.sum(p, axis=0, keepdims=True)
        acc = jnp.sum(rnd(p) * vn, axis=0, keepdims=True)
        for i in range(n_pg):
            p = jnp.exp(s_pg[i] - m)
            l = l + jnp.sum(p, axis=0, keepdims=True)
            acc = acc + jnp.sum(rnd(p) * rnd(vbuf[slot, qq * n_pg + i]), axis=0, keepdims=True)
        o_ref[qq:qq + 1, :] = (acc / l).astype(o_ref.dtype)


def moba_sample(q, k_new, v_new, cache_k, cache_v, layer, page_table, heads):
    r, s, d = q.shape
    hd = d // heads
    page = cache_k.shape[2]
    npg = page_table.shape[1]
    ppb = MOBA_BLOCK // page
    nbp = npg // ppb
    sp = 8
    assert MOBA_BLOCK % page == 0 and npg % ppb == 0 and s <= sp
    n_sel = min(MOBA_TOPK, nbp)
    kmean = block_sums(cache_k, layer, page_table, ppb) / float(MOBA_BLOCK)
    gate = jnp.einsum('rqhd,rnhd->rhqn', q.reshape(r, s, heads, hd).astype(BF16), kmean.astype(BF16),
                      preferred_element_type=F32)
    top_v, top_i = lax.top_k(gate, n_sel)
    ok = jnp.isfinite(top_v).astype(jnp.int32).reshape(-1)
    pages = page_table[jnp.arange(r)[:, None, None, None, None],
                       top_i[..., None] * ppb + jnp.arange(ppb)]
    pid = pages.reshape(-1).astype(jnp.int32)
    n_pg = n_sel * ppb
    pad = lambda t: jnp.pad(t, ((0, 0), (0, sp - s), (0, 0)))
    head_blk = lambda rows: pl.BlockSpec((None, rows, hd), lambda b, h, pid, ok: (b, 0, h))
    grid_spec = pltpu.PrefetchScalarGridSpec(
        num_scalar_prefetch=2, grid=(r, heads),
        in_specs=[head_blk(s), head_blk(sp), head_blk(sp),
                  pl.BlockSpec(memory_space=pl.ANY), pl.BlockSpec(memory_space=pl.ANY)],
        out_specs=head_blk(s),
        scratch_shapes=[pltpu.VMEM((2, s * n_pg, page, hd), F32), pltpu.VMEM((2, s * n_pg, page, hd), F32),
                        pltpu.SemaphoreType.DMA((2, 2))])
    return pl.pallas_call(
        functools.partial(_moba_sample_kernel, layer=layer, n_sel=n_sel, ppb=ppb, scale=float(hd) ** -0.5),
        grid_spec=grid_spec,
        out_shape=jax.ShapeDtypeStruct((r, s, d), BF16),
        compiler_params=_cp("arbitrary", "arbitrary"),
        name="moba_sample",
    )(pid, ok, q, pad(k_new), pad(v_new), cache_k, cache_v)


def _pool_kernel(u_ref, prev_ref, hist_ref, w_ref, sc_ref, o_ref, ext_ref, *, has_hist, wmax):
    g = pl.program_id(0)
    i = pl.program_id(2)
    ts = u_ref.shape[0]
    u = u_ref[...]
    ext_ref[0:wmax, :] = jnp.where(i == 0, hist_ref[...], prev_ref[...])
    ext_ref[wmax:wmax + ts, :] = u
    win = jnp.left_shift(2, g)
    acc = u
    for j in range(1, wmax):
        acc = acc + jnp.where(j < win, ext_ref[wmax - j:wmax - j + ts, :], 0.0)
    winf = win.astype(F32)
    if has_hist:
        cnt = jnp.full((ts, 1), 1.0, F32) * winf
    else:
        row = lax.broadcasted_iota(jnp.int32, (ts, 1), 0) + i * ts
        cnt = jnp.minimum((row + 1).astype(F32), winf)
    d = (acc / cnt - u).astype(BF16)
    o_ref[...] = jnp.dot(d, w_ref[...].astype(BF16), preferred_element_type=F32) * sc_ref[...]


def pool_mix(u, hist, pool_w, layer, pool_scale):
    b, s, d = u.shape
    ng = pool_w.shape[1]
    gsz = d // ng
    wmax = max(POOL_WINDOWS)
    assert POOL_WINDOWS == tuple(2 << g for g in range(ng)) and hist.shape[1] == wmax
    ts = min(POOL_TS, s)
    prev_arr = u if s >= wmax else hist
    r16 = ts // wmax if s >= wmax else 0
    return pl.pallas_call(
        functools.partial(_pool_kernel, has_hist=s < wmax, wmax=wmax),
        grid=(ng, b, s // ts),
        in_specs=[pl.BlockSpec((None, ts, gsz), lambda g, bi, i: (bi, i, g)),
                  pl.BlockSpec((None, wmax, gsz), lambda g, bi, i: (bi, jnp.maximum(i * r16 - 1, 0), g)),
                  pl.BlockSpec((None, wmax, gsz), lambda g, bi, i: (bi, 0, g)),
                  pl.BlockSpec((None, None, gsz, gsz), lambda g, bi, i: (layer, g, 0, 0)),
                  pl.BlockSpec((1, gsz), lambda g, bi, i: (0, g))],
        out_specs=pl.BlockSpec((None, ts, gsz), lambda g, bi, i: (bi, i, g)),
        out_shape=jax.ShapeDtypeStruct((b, s, d), F32),
        scratch_shapes=[pltpu.VMEM((wmax + ts, gsz), F32)],
        compiler_params=_cp("parallel", "parallel", "parallel"),
        name="pool_mix",
    )(u, prev_arr, hist, pool_w, pool_scale.reshape(1, d))


def _first_of_expert(be_ref, blk):
    return (blk == 0) | (be_ref[blk] != be_ref[jnp.maximum(blk - 1, 0)])


def _stream_expert_weights(be_ref, nx_ref, copies, cast):
    j, blk = pl.program_id(0), pl.program_id(1)
    nj = pl.num_programs(0)

    @pl.when((j == 0) & (blk == 0))
    def _():
        for c in copies(j, be_ref[blk]):
            c.start()

    @pl.when(_first_of_expert(be_ref, blk))
    def _():
        for c in copies(j, be_ref[blk]):
            c.wait()
        cast()
        nxt = nx_ref[blk]

        @pl.when(nxt >= 0)
        def _():
            for c in copies(j, nxt):
                c.start()

        @pl.when((nxt < 0) & (j + 1 < nj))
        def _():
            for c in copies(j + 1, be_ref[0]):
                c.start()


def _moe_gu_kernel(be_ref, nx_ref, nu_ref, x_ref, bg_ref, bu_ref, w_ref, h_ref, stg_ref, stu_ref, wgb_ref, wub_ref,
                   sem, *, layer):
    blk = pl.program_id(1)
    tn = stg_ref.shape[1]
    f = w_ref.shape[3] // 2

    def copies(jj, e):
        cg = pl.ds(pl.multiple_of(jj * tn, tn), tn)
        cu = pl.ds(pl.multiple_of(f + jj * tn, tn), tn)
        return (pltpu.make_async_copy(w_ref.at[layer, e, :, cg], stg_ref, sem.at[0]),
                pltpu.make_async_copy(w_ref.at[layer, e, :, cu], stu_ref, sem.at[1]))

    def cast():
        wgb_ref[...] = stg_ref[...].astype(BF16)
        wub_ref[...] = stu_ref[...].astype(BF16)

    _stream_expert_weights(be_ref, nx_ref, copies, cast)

    @pl.when(blk < nu_ref[0])
    def _():
        x = x_ref[...]
        g = jnp.dot(x, wgb_ref[...], preferred_element_type=F32) + bg_ref[...]
        up = jnp.dot(x, wub_ref[...], preferred_element_type=F32) + bu_ref[...]
        g = jnp.minimum(g, SWIGLU_LIMIT)
        up = jnp.clip(up, -SWIGLU_LIMIT, SWIGLU_LIMIT)
        h_ref[...] = (g * jax.nn.sigmoid(SWIGLU_ALPHA * g) * (up + 1.0)).astype(h_ref.dtype)

    @pl.when(blk >= nu_ref[0])
    def _():
        h_ref[...] = jnp.zeros(h_ref.shape, h_ref.dtype)


def _moe_down_kernel(be_ref, nx_ref, nu_ref, h_ref, b_ref, w_ref, o_ref, st_ref, wb_ref, sem, *, layer):
    blk = pl.program_id(1)
    tn = st_ref.shape[1]

    def copies(jj, e):
        return (pltpu.make_async_copy(w_ref.at[layer, e, :, pl.ds(pl.multiple_of(jj * tn, tn), tn)], st_ref, sem.at[0]),)

    def cast():
        wb_ref[...] = st_ref[...].astype(BF16)

    _stream_expert_weights(be_ref, nx_ref, copies, cast)

    @pl.when(blk < nu_ref[0])
    def _():
        o_ref[...] = (jnp.dot(h_ref[...], wb_ref[...], preferred_element_type=F32) + b_ref[...]).astype(o_ref.dtype)

    @pl.when(blk >= nu_ref[0])
    def _():
        o_ref[...] = jnp.zeros(o_ref.shape, o_ref.dtype)


def moe_experts(xs, blk_exp, blk_next, n_used, w_gu, b_gu, w_down, b_down, layer):
    p, d = xs.shape
    ne, _, f2 = w_gu.shape[1:]
    f = f2 // 2
    tm = MOE_TM
    nblk = p // tm
    tn = min(MOE_TN, f)
    tn2 = min(MOE_TN2, d)
    last = lambda blk, nu: jnp.minimum(blk, nu[0] - 1)
    hbm = pl.BlockSpec(memory_space=pl.ANY)
    gs1 = pltpu.PrefetchScalarGridSpec(
        num_scalar_prefetch=3, grid=(f // tn, nblk),
        in_specs=[pl.BlockSpec((tm, d), lambda j, i, be, nx, nu: (last(i, nu), 0)),
                  pl.BlockSpec((None, None, 1, tn), lambda j, i, be, nx, nu: (layer, be[i], 0, j)),
                  pl.BlockSpec((None, None, 1, tn), lambda j, i, be, nx, nu: (layer, be[i], 0, f // tn + j)),
                  hbm],
        out_specs=pl.BlockSpec((tm, tn), lambda j, i, be, nx, nu: (i, j)),
        scratch_shapes=[pltpu.VMEM((d, tn), F32), pltpu.VMEM((d, tn), F32),
                        pltpu.VMEM((d, tn), BF16), pltpu.VMEM((d, tn), BF16), pltpu.SemaphoreType.DMA((2,))])
    b_gu4 = b_gu.reshape(b_gu.shape[0], ne, 1, f2)
    h = pl.pallas_call(
        functools.partial(_moe_gu_kernel, layer=layer), grid_spec=gs1,
        out_shape=jax.ShapeDtypeStruct((p, f), BF16),
        compiler_params=_cp("arbitrary", "arbitrary"),
        name="moe_gate_up",
    )(blk_exp, blk_next, n_used, xs, b_gu4, b_gu4, w_gu)
    gs2 = pltpu.PrefetchScalarGridSpec(
        num_scalar_prefetch=3, grid=(d // tn2, nblk),
        in_specs=[pl.BlockSpec((tm, f), lambda j, i, be, nx, nu: (last(i, nu), 0)),
                  pl.BlockSpec((None, None, 1, tn2), lambda j, i, be, nx, nu: (layer, be[i], 0, j)),
                  hbm],
        out_specs=pl.BlockSpec((tm, tn2), lambda j, i, be, nx, nu: (i, j)),
        scratch_shapes=[pltpu.VMEM((f, tn2), F32), pltpu.VMEM((f, tn2), BF16), pltpu.SemaphoreType.DMA((1,))])
    return pl.pallas_call(
        functools.partial(_moe_down_kernel, layer=layer), grid_spec=gs2,
        out_shape=jax.ShapeDtypeStruct((p, d), BF16),
        compiler_params=_cp("arbitrary", "arbitrary"),
        name="moe_down",
    )(blk_exp, blk_next, n_used, h, b_down.reshape(b_down.shape[0], ne, 1, d), w_down)


def moe_route(logits, n_exp):
    t = logits.shape[0]
    tm = MOE_TM
    a = t * TOP_K
    top_v, top_i = lax.top_k(logits, TOP_K)
    gates = jax.nn.softmax(top_v, axis=-1)
    experts = jnp.arange(n_exp, dtype=jnp.int32)
    hit = top_i[:, :, None] == experts
    chosen = jnp.sum(hit, axis=1).astype(F32)
    counts = jnp.sum(chosen, axis=0).astype(jnp.int32)
    padded = (counts + tm - 1) // tm * tm
    pad_end = jnp.cumsum(padded)
    pad_start = pad_end - padded
    start = jnp.cumsum(counts) - counts
    ch = 256
    nch = -(-t // ch)
    cp = jnp.pad(chosen, ((0, nch * ch - t), (0, 0))).reshape(nch, ch, n_exp)
    tri = (jnp.arange(ch)[:, None] > jnp.arange(ch)[None, :]).astype(F32)
    within = jnp.einsum('ij,cje->cie', tri, cp)
    tot = jnp.sum(cp, axis=1)
    before = (within + (jnp.cumsum(tot, axis=0) - tot)[:, None, :]).reshape(nch * ch, n_exp)[:t]
    rank = jnp.sum(jnp.where(hit, before[:, None, :], 0.0), axis=-1).astype(jnp.int32)
    pos = jnp.sum(jnp.where(hit, pad_start, 0), axis=-1).astype(jnp.int32) + rank
    flat_e = top_i.reshape(-1).astype(jnp.int32)
    keys = jnp.sort(flat_e * a + jnp.arange(a, dtype=jnp.int32))
    order = keys - (keys // a) * a
    nblk = -(-a // tm) + n_exp
    blk_first = jnp.arange(nblk, dtype=jnp.int32) * tm
    blk_exp = jnp.minimum(jnp.sum(pad_end[None, :] <= blk_first[:, None], axis=1), n_exp - 1).astype(jnp.int32)
    n_used = (pad_end[-1] // tm).astype(jnp.int32)
    r0 = blk_first - pad_start[blk_exp]
    r = r0[:, None] + jnp.arange(tm, dtype=jnp.int32)[None, :]
    valid = (r < counts[blk_exp][:, None]) & (jnp.arange(nblk)[:, None] < n_used)
    src = jnp.clip(start[blk_exp][:, None] + r, 0, a - 1)
    slot_tok = jnp.where(valid, jnp.take(order, src, mode="clip") // TOP_K, t).reshape(-1).astype(jnp.int32)
    blk_exp = jnp.where(jnp.arange(nblk) < n_used, blk_exp, blk_exp[n_used - 1]).astype(jnp.int32)
    later = jnp.where((experts[None, :] > experts[:, None]) & (counts[None, :] > 0), experts[None, :], n_exp)
    nxt = jnp.min(later, axis=1)
    blk_next = jnp.where(nxt < n_exp, nxt, -1)[blk_exp].astype(jnp.int32)
    return gates, slot_tok, pos, blk_exp, blk_next, n_used.reshape(1)


def _pad_seq(t, n):
    return t if t.shape[1] == n else jnp.pad(t, ((0, 0), (0, n - t.shape[1]), (0, 0)))


def _mlstm_layer(u, gates_pre, w_in, norm_g, w_out, j, carry):
    b, s, d = u.shape
    c0, n0, m0 = carry
    h_, dk, dv = c0.shape[1:]
    L = ML_L if s % ML_L == 0 else -(-s // ML_L_MIN) * ML_L_MIN
    sp = -(-s // L) * L
    if sp != s:
        u = _pad_seq(u, sp)
        padg = jnp.concatenate([jnp.full((b, sp - s, h_), NEG, F32), jnp.full((b, sp - s, h_), -NEG, F32)], -1)
        gates_pre = jnp.concatenate([gates_pre, padg], 1)
    nqkv = 2 * h_ * dk + h_ * dv
    u2 = u.reshape(b * sp, d)
    proj = matmul(u2, w_in, j, 0, nqkv, BF16)
    ogate = matmul(u2, w_in, j, nqkv, h_ * dv, F32)
    hg, c_f, n_f, m_f = mlstm_core(proj, ogate, gates_pre, norm_g, c0, n0, m0, b, sp, L)
    y = matmul(hg, w_out, j, 0, d, F32).reshape(b, sp, d)
    return y[:, :s], (c_f, n_f, m_f)


def _moba_layer(u, w_qkv, w_out, j, heads, past):
    b, s, d = u.shape
    hd = d // heads
    x2 = u.reshape(b * s, d)
    q, k, v = (matmul(x2, w_qkv, j, c * d, d, F32).reshape(b, s, d) for c in range(3))
    p0 = 0 if past is None else past[2].shape[1] * past[0].shape[2]
    tabs = rope_tables(p0 + jnp.arange(s), hd)
    q, k = rope(q, tabs, hd), rope(k, tabs, hd)
    if past is None:
        o = moba_prompt(q.reshape(b * s, d), k.reshape(b * s, d), v.reshape(b * s, d), b, s, heads)
    else:
        cache_k, cache_v, page_table = past
        o = moba_sample(q, k, v, cache_k, cache_v, j, page_table, heads).reshape(b * s, d)
    y = matmul(o, w_out, j, 0, d, F32).reshape(b, s, d)
    return y, (k.reshape(b, s, heads, hd), v.reshape(b, s, heads, hd))


def kernel(x_prompt, x_sample, c_prompt, c_sample, state_mlstm_C, state_mlstm_n, state_mlstm_m, cache_moba_k, cache_moba_v, page_table, state_pool, ada_w, ada_b, ln_g, ln_b, mlstm_w_in, mlstm_b_gates, mlstm_norm_g, mlstm_w_out, moba_w_qkv, moba_w_out, pool_w, pool_scale, router_w, router_b, moe_w_gu, moe_b_gu, moe_w_down, moe_b_down):
    depth, d = ada_w.shape[0], ada_w.shape[1]
    bp, sp_, _ = x_prompt.shape
    bs, ss, _ = x_sample.shape
    n_exp = router_w.shape[2]
    ml_heads, ml_dk, ml_dv = state_mlstm_C.shape[2:]
    moba_heads = cache_moba_k.shape[3]
    alpha = (2.0 * depth) ** 0.25
    nproj = 2 * ml_heads * (ml_dk + ml_dv)

    nreq = bp + bs
    rpad = -(-nreq // 8) * 8
    c_all = jnp.pad(jnp.concatenate([c_prompt, c_sample], 0), ((0, rpad - nreq), (0, 0)))
    mods_all = ada_mod(c_all, ada_w, ada_b)
    groups = [dict(x=x_prompt, lo=0, b=bp), dict(x=x_sample, lo=bp, b=bs)]

    def mods_of(i, grp):
        return mods_all[i, grp['lo']:grp['lo'] + grp['b']].reshape(grp['b'], 1, 6 * d)

    def mixer_aux(i):
        if i % N_MIXERS == 0:
            jj = i // N_MIXERS
            return mlstm_w_in[jj][:, nproj:], mlstm_b_gates[jj]
        return None, None

    def u_dtype(i):
        return F32 if i % N_MIXERS == 2 else BF16

    aw, ab = mixer_aux(0)
    for grp in groups:
        outs = ln_mod(grp['x'], mods_of(0, grp), mod_c=(0, 1), aux_w=aw, aux_b=ab, u_dtype=u_dtype(0))
        grp['u'] = outs[0]
        grp['aux'] = outs[1] if aw is not None else None

    ml, mb, plst = ([], []), ([], []), ([], [])
    for i in range(depth):
        kind, j = i % N_MIXERS, i // N_MIXERS
        ys = []
        for gi, grp in enumerate(groups):
            u = grp['u']
            b, s, _ = u.shape
            if kind == 0:
                if gi == 0:
                    carry = (jnp.zeros((b, ml_heads, ml_dk, ml_dv), F32), jnp.zeros((b, ml_heads, ml_dk), F32),
                             jnp.full((b, ml_heads), ML_M_INIT, F32))
                else:
                    carry = (state_mlstm_C[j].astype(F32), state_mlstm_n[j].astype(F32), state_mlstm_m[j].astype(F32))
                y, st = _mlstm_layer(u, grp['aux'], mlstm_w_in, mlstm_norm_g[j], mlstm_w_out, j, carry)
                ml[gi].append(st)
            elif kind == 1:
                past = None if gi == 0 else (cache_moba_k, cache_moba_v, page_table)
                y, st = _moba_layer(u, moba_w_qkv, moba_w_out, j, moba_heads, past)
                mb[gi].append(st)
            else:
                wmax = max(POOL_WINDOWS)
                if gi == 0:
                    hist = jnp.zeros((b, wmax, d), F32)
                    ext = u
                else:
                    hist = jnp.pad(state_pool[j].astype(F32), ((0, 0), (1, 0), (0, 0)))
                    ext = jnp.concatenate([state_pool[j].astype(F32), u], 1)
                y = pool_mix(u, hist, pool_w, j, pool_scale[j])
                plst[gi].append(ext[:, ext.shape[1] - (wmax - 1):])
            ys.append(y)

        for grp, y in zip(groups, ys):
            grp['x'], grp['u'], grp['aux'] = ln_mod(
                grp['x'], mods_of(i, grp), y=y, gate_c=2, ln_g=ln_g[i, 0], ln_b=ln_b[i, 0], mod_c=(3, 4),
                aux_w=router_w[i], aux_b=router_b[i], u_dtype=BF16, alpha=alpha)

        toks = [g_['u'].reshape(-1, d) for g_ in groups]
        logits = jnp.concatenate([g_['aux'].reshape(-1, n_exp) for g_ in groups], 0)
        t_all = logits.shape[0]
        gates, slot_tok, pos, blk_exp, blk_next, n_used = moe_route(logits, n_exp)
        x_pad = jnp.concatenate(toks + [jnp.zeros((1, d), BF16)], 0)
        xs = jnp.take(x_pad, slot_tok, axis=0, mode="clip")
        ys_slots = moe_experts(xs, blk_exp, blk_next, n_used, moe_w_gu, moe_b_gu, moe_w_down, moe_b_down, i)

        last = i == depth - 1
        aw, ab = (None, None) if last else mixer_aux(i + 1)
        t0 = 0
        for grp in groups:
            b, s, _ = grp['x'].shape
            n = b * s
            pg = pos[t0:t0 + n]
            ysg = jnp.take(ys_slots, pg.T.reshape(-1), axis=0, mode="clip").reshape(TOP_K, b, s, d)
            gt = gates[t0:t0 + n].reshape(b, s, TOP_K)
            t0 += n
            outs = ln_mod(grp['x'], mods_of(i, grp) if last else jnp.concatenate([mods_of(i, grp), mods_of(i + 1, grp)], -1),
                          ys=ysg, gates=gt, gate_c=5, ln_g=ln_g[i, 1], ln_b=ln_b[i, 1],
                          mod_c=None if last else (6, 7), aux_w=aw, aux_b=ab,
                          u_dtype=BF16 if last else u_dtype(i + 1), alpha=alpha)
            grp['x'] = outs[0]
            if not last:
                grp['u'] = outs[1]
                grp['aux'] = outs[2] if aw is not None else None

    xdt = x_prompt.dtype
    sdt = state_mlstm_C.dtype
    stack = lambda sts, k, dt: jnp.stack([s_[k] for s_ in sts]).astype(dt)
    return (groups[0]['x'], groups[1]['x'],
            stack(ml[0], 0, xdt), stack(ml[0], 1, xdt), stack(ml[0], 2, xdt),
            stack(ml[1], 0, sdt), stack(ml[1], 1, sdt), stack(ml[1], 2, sdt),
            stack(mb[0], 0, xdt), stack(mb[0], 1, xdt), stack(mb[1], 0, xdt), stack(mb[1], 1, xdt),
            jnp.stack(plst[0]), jnp.stack(plst[1]))
```

```python
import functools

import jax
import jax.numpy as jnp
from jax import lax
from jax.experimental import pallas as pl
from jax.experimental.pallas import tpu as pltpu

F32 = jnp.float32
BF16 = jnp.bfloat16

N_MIXERS = 3
ML_M_INIT = -1.0e30
ML_CHUNK = 128
MOBA_BLOCK = 256
MOBA_TOPK = 3
ROPE_THETA = 500000.0
POOL_WINDOWS = (2, 4, 8, 16)
TOP_K = 4
SWIGLU_LIMIT = 7.0
SWIGLU_ALPHA = 1.702
LN_EPS = 1e-5
RMS_EPS = 1e-6

LANES = 128
VMEM_LIMIT = 56 * 1024 * 1024
MM_TM = 1024
MM_TN = 512
LN_TS = 128
MOE_TM = 256
MOE_TN = 512
MOE_TN2 = 2048
ML_L = 256
ML_L_MIN = 16
POOL_TS = 256
NEG = -1.0e30


def _cp(*sem):
    return pltpu.CompilerParams(dimension_semantics=sem, vmem_limit_bytes=VMEM_LIMIT)


def _ada_kernel(c_ref, w_ref, b_ref, o_ref):
    c = c_ref[...]
    a = (c * jax.nn.sigmoid(c)).astype(BF16)
    o_ref[...] = jnp.dot(a, w_ref[...].astype(BF16), preferred_element_type=F32) + b_ref[...]


def ada_mod(c_all, ada_w, ada_b):
    depth, d, n = ada_w.shape
    r = c_all.shape[0]
    tn = min(MM_TN, n)
    return pl.pallas_call(
        _ada_kernel,
        grid=(depth, n // tn),
        in_specs=[pl.BlockSpec((r, d), lambda l, j: (0, 0)),
                  pl.BlockSpec((None, d, tn), lambda l, j: (l, 0, j)),
                  pl.BlockSpec((None, 1, tn), lambda l, j: (l, 0, j))],
        out_specs=pl.BlockSpec((None, r, tn), lambda l, j: (l, 0, j)),
        out_shape=jax.ShapeDtypeStruct((depth, r, n), F32),
        compiler_params=_cp("arbitrary", "arbitrary"),
        name="ada_mod",
    )(c_all, ada_w, ada_b.reshape(depth, 1, n))


def _ln_mod_kernel(*refs, res, has_ln, has_mod, has_aux, alpha):
    it = iter(refs)
    x_ref = next(it)
    x = x_ref[...]
    if has_ln:
        if res == "dense":
            y = next(it)[...].astype(F32)
        else:
            ys_ref = next(it)
            gt = next(it)[...]
            y = gt[:, 0:1] * ys_ref[0].astype(F32)
            for k in range(1, ys_ref.shape[0]):
                y = y + gt[:, k:k + 1] * ys_ref[k].astype(F32)
        gate = next(it)[...]
        lg = next(it)[...]
        lb = next(it)[...]
        z = alpha * x + gate * y
        mu = jnp.mean(z, axis=-1, keepdims=True)
        zc = z - mu
        var = jnp.mean(zc * zc, axis=-1, keepdims=True)
        x = zc * lax.rsqrt(var + LN_EPS) * lg + lb
    if has_mod:
        sc = next(it)[...]
        sh = next(it)[...]
        u = x * (1.0 + sc) + sh
    if has_aux:
        wa = next(it)[...]
        ba = next(it)[...]
    if has_ln:
        next(it)[...] = x
    if has_mod:
        u_ref = next(it)
        u_ref[...] = u.astype(u_ref.dtype)
    if has_aux:
        next(it)[...] = jnp.dot(u.astype(BF16), wa.astype(BF16), preferred_element_type=F32) + ba


def ln_mod(x, mods, *, y=None, ys=None, gates=None, gate_c=None, ln_g=None, ln_b=None,
           mod_c=None, aux_w=None, aux_b=None, u_dtype=BF16, alpha=1.0):
    b, s, d = x.shape
    ts = min(LN_TS, s)
    grid = (b, s // ts)
    row = pl.BlockSpec((None, ts, d), lambda i, j: (i, j, 0))

    def modspec(c):
        return pl.BlockSpec((None, 1, d), lambda i, j, c=c: (i, 0, c))

    vec = pl.BlockSpec((1, d), lambda i, j: (0, 0))
    has_ln = gate_c is not None
    has_mod = mod_c is not None
    has_aux = aux_w is not None
    res = "dense" if y is not None else "moe"
    args, specs = [x], [row]
    if has_ln:
        if y is not None:
            args.append(y)
            specs.append(row)
        else:
            k = ys.shape[0]
            args += [ys, gates]
            specs += [pl.BlockSpec((k, None, ts, d), lambda i, j: (0, i, j, 0)),
                      pl.BlockSpec((None, ts, k), lambda i, j: (i, j, 0))]
        args += [mods, ln_g.reshape(1, d), ln_b.reshape(1, d)]
        specs += [modspec(gate_c), vec, vec]
    if has_mod:
        args += [mods, mods]
        specs += [modspec(mod_c[1]), modspec(mod_c[0])]
    if has_aux:
        na = aux_w.shape[1]
        args += [aux_w, aux_b.reshape(1, na)]
        specs += [pl.BlockSpec((d, na), lambda i, j: (0, 0)), pl.BlockSpec((1, na), lambda i, j: (0, 0))]
    out_shape, out_specs = [], []
    if has_ln:
        out_shape.append(jax.ShapeDtypeStruct((b, s, d), F32))
        out_specs.append(row)
    if has_mod:
        out_shape.append(jax.ShapeDtypeStruct((b, s, d), u_dtype))
        out_specs.append(row)
    if has_aux:
        out_shape.append(jax.ShapeDtypeStruct((b, s, na), F32))
        out_specs.append(pl.BlockSpec((None, ts, na), lambda i, j: (i, j, 0)))
    return pl.pallas_call(
        functools.partial(_ln_mod_kernel, res=res, has_ln=has_ln, has_mod=has_mod, has_aux=has_aux, alpha=alpha),
        grid=grid, in_specs=specs, out_specs=out_specs, out_shape=out_shape,
        compiler_params=_cp("parallel", "parallel"),
        name="ln_mod",
    )(*args)


def _mm_kernel(x_ref, w_ref, o_ref, wb_ref):
    @pl.when(pl.program_id(1) == 0)
    def _():
        wb_ref[...] = w_ref[...].astype(BF16)

    o_ref[...] = jnp.dot(x_ref[...], wb_ref[...], preferred_element_type=F32).astype(o_ref.dtype)


def matmul(x, w, layer, col0, n, out_dtype):
    m, k = x.shape
    tm = min(MM_TM, m)
    tn = min(MM_TN, n)
    assert m % tm == 0 and n % tn == 0 and col0 % tn == 0
    c0 = col0 // tn
    return pl.pallas_call(
        _mm_kernel,
        grid=(n // tn, m // tm),
        in_specs=[pl.BlockSpec((tm, k), lambda j, i: (i, 0)),
                  pl.BlockSpec((None, k, tn), lambda j, i: (layer, 0, c0 + j))],
        out_specs=pl.BlockSpec((tm, tn), lambda j, i: (i, j)),
        out_shape=jax.ShapeDtypeStruct((m, n), out_dtype),
        scratch_shapes=[pltpu.VMEM((k, tn), BF16)],
        compiler_params=_cp("arbitrary", "arbitrary"),
        name="matmul",
    )(x, w)


def _split3(x):
    hi = x.astype(BF16)
    r1 = x - hi.astype(F32)
    mid = r1.astype(BF16)
    lo = (r1 - mid.astype(F32)).astype(BF16)
    return hi, mid, lo


def _log_sigmoid(x):
    return jnp.minimum(x, 0.0) - jnp.log(1.0 + jnp.exp(-jnp.abs(x)))


def _mlstm_kernel(q_ref, k_ref, v_ref, o_ref, gr_ref, gc_ref, ng_ref, c0_ref, n0_ref, m0_ref,
                  h_ref, co_ref, no_ref, mo_ref, c_sc, n_sc, m_sc, *, scale):
    c_idx = pl.program_id(2)
    nc = pl.num_programs(2)
    L = q_ref.shape[0]

    @pl.when(c_idx == 0)
    def _():
        c_sc[...] = c0_ref[...]
        n_sc[...] = n0_ref[...]
        m_sc[...] = m0_ref[...]

    q = q_ref[...]
    k = k_ref[...]
    v = v_ref[...]
    gr = gr_ref[...]
    gc = gc_ref[...]
    logi_r = gr[0:1, :]
    logf_r = _log_sigmoid(gr[1:2, :])
    logi_c = gc[:, 0:1]
    logf_c = _log_sigmoid(gc[:, 1:2])
    ri = lax.broadcasted_iota(jnp.int32, (L, L), 0)
    ci = lax.broadcasted_iota(jnp.int32, (L, L), 1)
    causal = ci <= ri
    upper = jnp.where(ri <= ci, 1.0, 0.0).astype(BF16)
    lower = jnp.where(ci <= ri, 1.0, 0.0).astype(BF16)
    fr = jnp.broadcast_to(logf_r, (8, L))
    fc = jnp.broadcast_to(logf_c, (L, LANES))
    b_r = sum(jnp.dot(p, upper, preferred_element_type=F32) for p in _split3(fr))[0:1, :]
    b_c = sum(jnp.dot(lower, p, preferred_element_type=F32) for p in _split3(fc))[:, 0:1]
    m_prev = m_sc[...]
    logw = jnp.where(causal, b_c - b_r + logi_r, -jnp.inf)
    m_inter = m_prev + b_c
    m_t = jnp.maximum(m_inter, jnp.max(logw, axis=-1, keepdims=True))
    s = lax.dot_general(q, k, (((1,), (1,)), ((), ())), preferred_element_type=F32)
    s = s * scale * jnp.exp(logw - m_t)
    inter = jnp.exp(m_inter - m_t)
    c_mat = c_sc[...]
    n_vec = n_sc[...]
    qf = q.astype(F32) * scale
    num = jnp.dot(s.astype(BF16), v, preferred_element_type=F32) + inter * (
        jnp.dot(q, c_mat.astype(BF16), preferred_element_type=F32) * scale)
    den = jnp.sum(s, axis=-1, keepdims=True) + inter * jnp.sum(
        qf * n_vec.astype(BF16).astype(F32), axis=-1, keepdims=True)
    h = num / jnp.maximum(jnp.abs(den), jnp.exp(-m_t))
    m_new = m_t[L - 1:L, :]
    b_last = b_c[L - 1:L, :]
    w_end = jnp.exp(b_last - b_c + logi_c - m_new)
    decay = jnp.exp(m_prev + b_last - m_new)
    kw = k.astype(F32) * w_end
    c_new = decay * c_mat + lax.dot_general(kw.astype(BF16), v, (((0,), (0,)), ((), ())),
                                            preferred_element_type=F32)
    n_new = decay * n_vec + jnp.sum(kw, axis=0, keepdims=True)
    c_sc[...] = c_new
    n_sc[...] = n_new
    m_sc[...] = m_new
    hn = h * lax.rsqrt(jnp.mean(h * h, axis=-1, keepdims=True) + RMS_EPS) * ng_ref[...]
    h_ref[...] = (jax.nn.sigmoid(o_ref[...].astype(F32)) * hn).astype(h_ref.dtype)

    @pl.when(c_idx == nc - 1)
    def _():
        co_ref[...] = c_new
        no_ref[...] = n_new
        mo_ref[...] = m_new


def mlstm_core(proj, ogate, gates, norm_g, c0, n0, m0, bsz, seq, L):
    h_, dk, dv = c0.shape[1:]
    nc = seq // L
    g = gates.reshape(bsz, nc, L, 2, h_)
    g_row = g.transpose(0, 4, 1, 3, 2)
    g_col = g.transpose(0, 4, 1, 2, 3)
    kq, kk, kv = 0, h_ * dk // dk, 2 * h_ * dk // dv
    row = lambda b, h, c: b * nc + c
    out = pl.pallas_call(
        functools.partial(_mlstm_kernel, scale=float(dk) ** -0.5),
        grid=(bsz, h_, nc),
        in_specs=[pl.BlockSpec((L, dk), lambda b, h, c: (row(b, h, c), kq + h)),
                  pl.BlockSpec((L, dk), lambda b, h, c: (row(b, h, c), kk + h)),
                  pl.BlockSpec((L, dv), lambda b, h, c: (row(b, h, c), kv + h)),
                  pl.BlockSpec((L, dv), lambda b, h, c: (row(b, h, c), h)),
                  pl.BlockSpec((None, None, None, 2, L), lambda b, h, c: (b, h, c, 0, 0)),
                  pl.BlockSpec((None, None, None, L, 2), lambda b, h, c: (b, h, c, 0, 0)),
                  pl.BlockSpec((1, dv), lambda b, h, c: (0, h)),
                  pl.BlockSpec((None, None, dk, dv), lambda b, h, c: (b, h, 0, 0)),
                  pl.BlockSpec((None, None, 1, dk), lambda b, h, c: (b, h, 0, 0)),
                  pl.BlockSpec((None, None, 1, 1), lambda b, h, c: (b, h, 0, 0))],
        out_specs=[pl.BlockSpec((L, dv), lambda b, h, c: (row(b, h, c), h)),
                   pl.BlockSpec((None, None, dk, dv), lambda b, h, c: (b, h, 0, 0)),
                   pl.BlockSpec((None, None, 1, dk), lambda b, h, c: (b, h, 0, 0)),
                   pl.BlockSpec((None, None, 1, 1), lambda b, h, c: (b, h, 0, 0))],
        out_shape=[jax.ShapeDtypeStruct((bsz * seq, h_ * dv), BF16),
                   jax.ShapeDtypeStruct((bsz, h_, dk, dv), F32),
                   jax.ShapeDtypeStruct((bsz, h_, 1, dk), F32),
                   jax.ShapeDtypeStruct((bsz, h_, 1, 1), F32)],
        scratch_shapes=[pltpu.VMEM((dk, dv), F32), pltpu.VMEM((1, dk), F32), pltpu.VMEM((1, 1), F32)],
        compiler_params=_cp("parallel", "parallel", "arbitrary"),
        name="mlstm",
    )(proj, proj, proj, ogate, g_row, g_col, norm_g.reshape(1, h_ * dv),
      c0, n0.reshape(bsz, h_, 1, dk), m0.reshape(bsz, h_, 1, 1))
    hg, c_f, n_f, m_f = out
    return hg, c_f, n_f.reshape(bsz, h_, dk), m_f.reshape(bsz, h_)


def rope_tables(pos, hd):
    rot = hd // 4
    half = rot // 2
    inv = ROPE_THETA ** (-jnp.arange(half, dtype=F32) * 2.0 / rot)
    ang = pos.astype(F32)[:, None] * inv[None, :]
    cos, sin = jnp.cos(ang), jnp.sin(ang)
    n = pos.shape[0]
    rest = hd - rot
    t_cos = jnp.concatenate([cos, cos, jnp.ones((n, rest), F32)], -1)
    t_up = jnp.concatenate([-sin, jnp.zeros((n, hd - half), F32)], -1)
    t_dn = jnp.concatenate([jnp.zeros((n, half), F32), sin, jnp.zeros((n, rest), F32)], -1)
    return t_cos, t_up, t_dn


def _rope_kernel(x_ref, c_ref, a_ref, b_ref, o_ref, *, hd, half):
    w = x_ref.shape[-1]
    x = x_ref[...]
    up = pltpu.roll(x, w - half, 1)
    dn = pltpu.roll(x, half, 1)
    c, a, b = c_ref[...], a_ref[...], b_ref[...]
    for h in range(w // hd):
        sl = slice(h * hd, (h + 1) * hd)
        o_ref[:, sl] = x[:, sl] * c + up[:, sl] * a + dn[:, sl] * b


def rope(x, tabs, hd):
    b, s, d = x.shape
    ts = min(256, s)
    w = min(d, 8 * hd)
    tab = pl.BlockSpec((ts, hd), lambda i, j, c: (j, 0))
    blk = pl.BlockSpec((None, ts, w), lambda i, j, c: (i, j, c))
    return pl.pallas_call(
        functools.partial(_rope_kernel, hd=hd, half=hd // 8),
        grid=(b, s // ts, d // w),
        in_specs=[blk, tab, tab, tab],
        out_specs=blk,
        out_shape=jax.ShapeDtypeStruct((b, s, d), F32),
        compiler_params=_cp("parallel", "parallel", "parallel"),
        name="rope",
    )(x, *tabs)


def _top_rows(gate, n_sel):
    row = lax.broadcasted_iota(jnp.int32, gate.shape, 0).astype(F32)
    sel = jnp.zeros(gate.shape, F32)
    g = gate
    for _ in range(n_sel):
        mx = jnp.max(g, axis=0, keepdims=True)
        first = jnp.min(jnp.where(g == mx, row, float(gate.shape[0])), axis=0, keepdims=True)
        hit = (row == first) & (mx > -jnp.inf)
        sel = jnp.where(hit, 1.0, sel)
        g = jnp.where(hit, -jnp.inf, g)
    return sel


def _moba_prompt_kernel(q_ref, k_ref, v_ref, o_ref, km_ref, kb_ref, vt_ref, *, blk, n_sel, scale, uses):
    qi = pl.program_id(2)
    seq = kb_ref.shape[0]
    nb = seq // blk

    @pl.when(qi == 0)
    def _():
        km_ref[...] = jnp.zeros(km_ref.shape, F32)
        for n in range(nb):
            kblk = k_ref[n * blk:(n + 1) * blk, :]
            km_ref[n:n + 1, :] = jnp.mean(kblk, axis=0, keepdims=True)
            kb_ref[n * blk:(n + 1) * blk, :] = kblk.astype(BF16)
            vt_ref[:, n * blk:(n + 1) * blk] = v_ref[n * blk:(n + 1) * blk, :].T.astype(BF16)

    qtb = q_ref[...].T.astype(BF16)
    gate = jnp.dot(km_ref[...].astype(BF16), qtb, preferred_element_type=F32)
    row = lax.broadcasted_iota(jnp.int32, gate.shape, 0)
    sel = _top_rows(jnp.where(row < qi, gate, -jnp.inf), n_sel)
    start = pl.multiple_of(qi * blk, blk)
    ki = lax.broadcasted_iota(jnp.int32, (blk, blk), 0)
    ci = lax.broadcasted_iota(jnp.int32, (blk, blk), 1)
    s_own = jnp.dot(kb_ref[pl.ds(start, blk), :], qtb, preferred_element_type=F32) * scale
    s_own = jnp.where(ki <= ci, s_own, NEG)

    def attend(nuse):
        s_all = jnp.dot(kb_ref[0:nuse * blk, :], qtb, preferred_element_type=F32) * scale
        s_blk = [jnp.where(sel[n:n + 1, :] > 0.0, s_all[n * blk:(n + 1) * blk, :], NEG) for n in range(nuse)]
        m = jnp.max(s_own, axis=0, keepdims=True)
        for sb in s_blk:
            m = jnp.maximum(m, jnp.max(sb, axis=0, keepdims=True))
        p_own = jnp.exp(s_own - m)
        l = jnp.sum(p_own, axis=0, keepdims=True)
        p_blk = [jnp.exp(sb - m) for sb in s_blk]
        for pb in p_blk:
            l = l + jnp.sum(pb, axis=0, keepdims=True)
        p_all = jnp.concatenate([pb.astype(BF16) for pb in p_blk], axis=0)
        acc = jnp.dot(vt_ref[:, pl.ds(start, blk)], p_own.astype(BF16), preferred_element_type=F32)
        acc = acc + jnp.dot(vt_ref[:, 0:nuse * blk], p_all, preferred_element_type=F32)
        o_ref[...] = (acc / l).T.astype(o_ref.dtype)

    lo = -1
    for nuse in uses:
        @pl.when((qi > lo) & (qi <= nuse))
        def _(nuse=nuse):
            attend(nuse)
        lo = nuse


def moba_prompt(q, k, v, bsz, seq, heads):
    hd = q.shape[1] // heads
    blk = MOBA_BLOCK
    assert seq % blk == 0
    nb = seq // blk
    nbp = -(-nb // 8) * 8
    return pl.pallas_call(
        functools.partial(_moba_prompt_kernel, blk=blk, n_sel=min(MOBA_TOPK, nb - 1), scale=float(hd) ** -0.5,
                          uses=tuple(sorted({max(1, -(-nb * f // 4)) for f in (1, 2, 3)} | {nb}))),
        grid=(bsz, heads, nb),
        in_specs=[pl.BlockSpec((blk, hd), lambda b, h, i: (b * nb + i, h)),
                  pl.BlockSpec((seq, hd), lambda b, h, i: (b, h)),
                  pl.BlockSpec((seq, hd), lambda b, h, i: (b, h))],
        out_specs=pl.BlockSpec((blk, hd), lambda b, h, i: (b * nb + i, h)),
        out_shape=jax.ShapeDtypeStruct((bsz * seq, heads * hd), BF16),
        scratch_shapes=[pltpu.VMEM((nbp, hd), F32), pltpu.VMEM((seq, hd), BF16), pltpu.VMEM((hd, seq), BF16)],
        compiler_params=_cp("parallel", "parallel", "arbitrary"),
        name="moba_prompt",
    )(q, k, v)


def _page_sum_kernel(pt_ref, *refs):
    o_ref = refs[-1]
    acc = jnp.sum(refs[0][...], axis=0)
    for k_ref in refs[1:-1]:
        acc = acc + jnp.sum(k_ref[...], axis=0)
    o_ref[...] = acc


def block_sums(cache, layer, page_table, ppb):
    _, _, page, heads, hd = cache.shape
    nreq, npg = page_table.shape
    nblk = npg // ppb

    def pg_spec(i):
        return pl.BlockSpec((None, None, page, heads, hd),
                            lambda b, n, pt, i=i: (layer, pt[b * npg + n * ppb + i], 0, 0, 0))

    grid_spec = pltpu.PrefetchScalarGridSpec(
        num_scalar_prefetch=1, grid=(nreq, nblk),
        in_specs=[pg_spec(i) for i in range(ppb)],
        out_specs=pl.BlockSpec((None, None, heads, hd), lambda b, n, pt: (b, n, 0, 0)))
    return pl.pallas_call(
        _page_sum_kernel, grid_spec=grid_spec,
        out_shape=jax.ShapeDtypeStruct((nreq, nblk, heads, hd), F32),
        compiler_params=_cp("parallel", "parallel"),
        name="block_sums",
    )(page_table.reshape(-1), *([cache] * ppb))


def _moba_sample_kernel(pid_ref, ok_ref, q_ref, kn_ref, vn_ref, ck_ref, cv_ref, o_ref, kbuf, vbuf, sem,
                        *, layer, n_sel, ppb, scale):
    nh = pl.num_programs(1)
    step = pl.program_id(0) * nh + pl.program_id(1)
    nsteps = pl.num_programs(0) * nh
    n_q = q_ref.shape[0]
    n_pg = n_sel * ppb
    per = n_q * n_pg

    def copies(st, slot):
        hh = st % nh
        out = []
        for j in range(per):
            pg = pid_ref[st * per + j]
            out.append(pltpu.make_async_copy(ck_ref.at[layer, pg, :, hh, :], kbuf.at[slot, j], sem.at[0, slot]))
            out.append(pltpu.make_async_copy(cv_ref.at[layer, pg, :, hh, :], vbuf.at[slot, j], sem.at[1, slot]))
        return out

    @pl.when(step == 0)
    def _():
        for c in copies(step, 0):
            c.start()

    @pl.when(step + 1 < nsteps)
    def _():
        for c in copies(step + 1, (step + 1) % 2):
            c.start()

    slot = step % 2
    for c in copies(step, slot):
        c.wait()

    rows = lax.broadcasted_iota(jnp.int32, (kn_ref.shape[0], 1), 0)
    rnd = lambda t: t.astype(BF16).astype(F32)
    kn = rnd(kn_ref[...])
    vn = rnd(vn_ref[...])
    for qq in range(n_q):
        q = rnd(q_ref[qq:qq + 1, :])
        s_new = jnp.sum(kn * q, axis=-1, keepdims=True) * scale
        s_new = jnp.where(rows <= qq, s_new, NEG)
        s_pg = []
        for i in range(n_pg):
            s = jnp.sum(rnd(kbuf[slot, qq * n_pg + i]) * q, axis=-1, keepdims=True) * scale
            s_pg.append(jnp.where(ok_ref[(step * n_q + qq) * n_sel + i // ppb] > 0, s, NEG))
        m = jnp.max(s_new, axis=0, keepdims=True)
        for s in s_pg:
            m = jnp.maximum(m, jnp.max(s, axis=0, keepdims=True))
        p = jnp.exp(s_new - m)
        l = jnp.sum(p, axis=0, keepdims=True)
        acc = jnp.sum(rnd(p) * vn, axis=0, keepdims=True)
        for i in range(n_pg):
            p = jnp.exp(s_pg[i] - m)
            l = l + jnp.sum(p, axis=0, keepdims=True)
            acc = acc + jnp.sum(rnd(p) * rnd(vbuf[slot, qq * n_pg + i]), axis=0, keepdims=True)
        o_ref[qq:qq + 1, :] = (acc / l).astype(o_ref.dtype)


def _top_k_small(x, k):
    n = x.shape[-1]
    idx = lax.broadcasted_iota(jnp.int32, x.shape, x.ndim - 1)
    vals, inds = [], []
    for _ in range(k):
        mx = jnp.max(x, axis=-1, keepdims=True)
        first = jnp.min(jnp.where(x == mx, idx, n), axis=-1, keepdims=True)
        vals.append(mx)
        inds.append(jnp.minimum(first, n - 1))
        x = jnp.where(idx == first, -jnp.inf, x)
    return jnp.concatenate(vals, -1), jnp.concatenate(inds, -1)


def moba_sample(q, k_new, v_new, cache_k, cache_v, layer, page_table, heads):
    r, s, d = q.shape
    hd = d // heads
    page = cache_k.shape[2]
    npg = page_table.shape[1]
    ppb = MOBA_BLOCK // page
    nbp = npg // ppb
    sp = 8
    assert MOBA_BLOCK % page == 0 and npg % ppb == 0 and s <= sp
    n_sel = min(MOBA_TOPK, nbp)
    kmean = block_sums(cache_k, layer, page_table, ppb) / float(MOBA_BLOCK)
    gate = jnp.einsum('rqhd,rnhd->rhqn', q.reshape(r, s, heads, hd).astype(BF16), kmean.astype(BF16),
                      preferred_element_type=F32)
    top_v, top_i = _top_k_small(gate, n_sel)
    ok = jnp.isfinite(top_v).astype(jnp.int32).reshape(-1)
    pages = page_table[jnp.arange(r)[:, None, None, None, None],
                       top_i[..., None] * ppb + jnp.arange(ppb)]
    pid = pages.reshape(-1).astype(jnp.int32)
    n_pg = n_sel * ppb
    pad = lambda t: jnp.pad(t, ((0, 0), (0, sp - s), (0, 0)))
    head_blk = lambda rows: pl.BlockSpec((None, rows, hd), lambda b, h, pid, ok: (b, 0, h))
    grid_spec = pltpu.PrefetchScalarGridSpec(
        num_scalar_prefetch=2, grid=(r, heads),
        in_specs=[head_blk(s), head_blk(sp), head_blk(sp),
                  pl.BlockSpec(memory_space=pl.ANY), pl.BlockSpec(memory_space=pl.ANY)],
        out_specs=head_blk(s),
        scratch_shapes=[pltpu.VMEM((2, s * n_pg, page, hd), F32), pltpu.VMEM((2, s * n_pg, page, hd), F32),
                        pltpu.SemaphoreType.DMA((2, 2))])
    return pl.pallas_call(
        functools.partial(_moba_sample_kernel, layer=layer, n_sel=n_sel, ppb=ppb, scale=float(hd) ** -0.5),
        grid_spec=grid_spec,
        out_shape=jax.ShapeDtypeStruct((r, s, d), BF16),
        compiler_params=_cp("arbitrary", "arbitrary"),
        name="moba_sample",
    )(pid, ok, q, pad(k_new), pad(v_new), cache_k, cache_v)


def _pool_kernel(u_ref, prev_ref, hist_ref, w_ref, sc_ref, o_ref, ext_ref, *, has_hist, wmax):
    g = pl.program_id(0)
    i = pl.program_id(2)
    ts = u_ref.shape[0]
    u = u_ref[...]
    ext_ref[0:wmax, :] = jnp.where(i == 0, hist_ref[...], prev_ref[...])
    ext_ref[wmax:wmax + ts, :] = u
    win = jnp.left_shift(2, g)
    acc = u
    for j in range(1, wmax):
        acc = acc + jnp.where(j < win, ext_ref[wmax - j:wmax - j + ts, :], 0.0)
    winf = win.astype(F32)
    if has_hist:
        cnt = jnp.full((ts, 1), 1.0, F32) * winf
    else:
        row = lax.broadcasted_iota(jnp.int32, (ts, 1), 0) + i * ts
        cnt = jnp.minimum((row + 1).astype(F32), winf)
    d = (acc / cnt - u).astype(BF16)
    o_ref[...] = jnp.dot(d, w_ref[...].astype(BF16), preferred_element_type=F32) * sc_ref[...]


def pool_mix(u, hist, pool_w, layer, pool_scale):
    b, s, d = u.shape
    ng = pool_w.shape[1]
    gsz = d // ng
    wmax = max(POOL_WINDOWS)
    assert POOL_WINDOWS == tuple(2 << g for g in range(ng)) and hist.shape[1] == wmax
    ts = min(POOL_TS, s)
    prev_arr = u if s >= wmax else hist
    r16 = ts // wmax if s >= wmax else 0
    return pl.pallas_call(
        functools.partial(_pool_kernel, has_hist=s < wmax, wmax=wmax),
        grid=(ng, b, s // ts),
        in_specs=[pl.BlockSpec((None, ts, gsz), lambda g, bi, i: (bi, i, g)),
                  pl.BlockSpec((None, wmax, gsz), lambda g, bi, i: (bi, jnp.maximum(i * r16 - 1, 0), g)),
                  pl.BlockSpec((None, wmax, gsz), lambda g, bi, i: (bi, 0, g)),
                  pl.BlockSpec((None, None, gsz, gsz), lambda g, bi, i: (layer, g, 0, 0)),
                  pl.BlockSpec((1, gsz), lambda g, bi, i: (0, g))],
        out_specs=pl.BlockSpec((None, ts, gsz), lambda g, bi, i: (bi, i, g)),
        out_shape=jax.ShapeDtypeStruct((b, s, d), F32),
        scratch_shapes=[pltpu.VMEM((wmax + ts, gsz), F32)],
        compiler_params=_cp("parallel", "parallel", "parallel"),
        name="pool_mix",
    )(u, prev_arr, hist, pool_w, pool_scale.reshape(1, d))


def _first_of_expert(be_ref, blk):
    return (blk == 0) | (be_ref[blk] != be_ref[jnp.maximum(blk - 1, 0)])


def _stream_expert_weights(be_ref, nx_ref, copies, cast):
    j, blk = pl.program_id(0), pl.program_id(1)
    nj = pl.num_programs(0)

    @pl.when((j == 0) & (blk == 0))
    def _():
        for c in copies(j, be_ref[blk]):
            c.start()

    @pl.when(_first_of_expert(be_ref, blk))
    def _():
        for c in copies(j, be_ref[blk]):
            c.wait()
        cast()
        nxt = nx_ref[blk]

        @pl.when(nxt >= 0)
        def _():
            for c in copies(j, nxt):
                c.start()

        @pl.when((nxt < 0) & (j + 1 < nj))
        def _():
            for c in copies(j + 1, be_ref[0]):
                c.start()


def _moe_gu_kernel(be_ref, nx_ref, nu_ref, x_ref, bg_ref, bu_ref, w_ref, h_ref, stg_ref, stu_ref, wgb_ref, wub_ref,
                   sem, *, layer):
    blk = pl.program_id(1)
    tn = stg_ref.shape[1]
    f = w_ref.shape[3] // 2

    def copies(jj, e):
        cg = pl.ds(pl.multiple_of(jj * tn, tn), tn)
        cu = pl.ds(pl.multiple_of(f + jj * tn, tn), tn)
        return (pltpu.make_async_copy(w_ref.at[layer, e, :, cg], stg_ref, sem.at[0]),
                pltpu.make_async_copy(w_ref.at[layer, e, :, cu], stu_ref, sem.at[1]))

    def cast():
        wgb_ref[...] = stg_ref[...].astype(BF16)
        wub_ref[...] = stu_ref[...].astype(BF16)

    _stream_expert_weights(be_ref, nx_ref, copies, cast)

    @pl.when(blk < nu_ref[0])
    def _():
        x = x_ref[...]
        g = jnp.dot(x, wgb_ref[...], preferred_element_type=F32) + bg_ref[...]
        up = jnp.dot(x, wub_ref[...], preferred_element_type=F32) + bu_ref[...]
        g = jnp.minimum(g, SWIGLU_LIMIT)
        up = jnp.clip(up, -SWIGLU_LIMIT, SWIGLU_LIMIT)
        h_ref[...] = (g * jax.nn.sigmoid(SWIGLU_ALPHA * g) * (up + 1.0)).astype(h_ref.dtype)

    @pl.when(blk >= nu_ref[0])
    def _():
        h_ref[...] = jnp.zeros(h_ref.shape, h_ref.dtype)


def _moe_down_kernel(be_ref, nx_ref, nu_ref, h_ref, b_ref, w_ref, o_ref, st_ref, wb_ref, sem, *, layer):
    blk = pl.program_id(1)
    tn = st_ref.shape[1]

    def copies(jj, e):
        return (pltpu.make_async_copy(w_ref.at[layer, e, :, pl.ds(pl.multiple_of(jj * tn, tn), tn)], st_ref, sem.at[0]),)

    def cast():
        wb_ref[...] = st_ref[...].astype(BF16)

    _stream_expert_weights(be_ref, nx_ref, copies, cast)

    @pl.when(blk < nu_ref[0])
    def _():
        o_ref[...] = (jnp.dot(h_ref[...], wb_ref[...], preferred_element_type=F32) + b_ref[...]).astype(o_ref.dtype)

    @pl.when(blk >= nu_ref[0])
    def _():
        o_ref[...] = jnp.zeros(o_ref.shape, o_ref.dtype)


def moe_experts(xs, blk_exp, blk_next, n_used, w_gu, b_gu, w_down, b_down, layer):
    p, d = xs.shape
    ne, _, f2 = w_gu.shape[1:]
    f = f2 // 2
    tm = MOE_TM
    nblk = p // tm
    tn = min(MOE_TN, f)
    tn2 = min(MOE_TN2, d)
    last = lambda blk, nu: jnp.minimum(blk, nu[0] - 1)
    hbm = pl.BlockSpec(memory_space=pl.ANY)
    gs1 = pltpu.PrefetchScalarGridSpec(
        num_scalar_prefetch=3, grid=(f // tn, nblk),
        in_specs=[pl.BlockSpec((tm, d), lambda j, i, be, nx, nu: (last(i, nu), 0)),
                  pl.BlockSpec((None, None, 1, tn), lambda j, i, be, nx, nu: (layer, be[i], 0, j)),
                  pl.BlockSpec((None, None, 1, tn), lambda j, i, be, nx, nu: (layer, be[i], 0, f // tn + j)),
                  hbm],
        out_specs=pl.BlockSpec((tm, tn), lambda j, i, be, nx, nu: (i, j)),
        scratch_shapes=[pltpu.VMEM((d, tn), F32), pltpu.VMEM((d, tn), F32),
                        pltpu.VMEM((d, tn), BF16), pltpu.VMEM((d, tn), BF16), pltpu.SemaphoreType.DMA((2,))])
    b_gu4 = b_gu.reshape(b_gu.shape[0], ne, 1, f2)
    h = pl.pallas_call(
        functools.partial(_moe_gu_kernel, layer=layer), grid_spec=gs1,
        out_shape=jax.ShapeDtypeStruct((p, f), BF16),
        compiler_params=_cp("arbitrary", "arbitrary"),
        name="moe_gate_up",
    )(blk_exp, blk_next, n_used, xs, b_gu4, b_gu4, w_gu)
    gs2 = pltpu.PrefetchScalarGridSpec(
        num_scalar_prefetch=3, grid=(d // tn2, nblk),
        in_specs=[pl.BlockSpec((tm, f), lambda j, i, be, nx, nu: (last(i, nu), 0)),
                  pl.BlockSpec((None, None, 1, tn2), lambda j, i, be, nx, nu: (layer, be[i], 0, j)),
                  hbm],
        out_specs=pl.BlockSpec((tm, tn2), lambda j, i, be, nx, nu: (i, j)),
        scratch_shapes=[pltpu.VMEM((f, tn2), F32), pltpu.VMEM((f, tn2), BF16), pltpu.SemaphoreType.DMA((1,))])
    return pl.pallas_call(
        functools.partial(_moe_down_kernel, layer=layer), grid_spec=gs2,
        out_shape=jax.ShapeDtypeStruct((p, d), BF16),
        compiler_params=_cp("arbitrary", "arbitrary"),
        name="moe_down",
    )(blk_exp, blk_next, n_used, h, b_down.reshape(b_down.shape[0], ne, 1, d), w_down)


def moe_route(logits, n_exp):
    t = logits.shape[0]
    tm = MOE_TM
    a = t * TOP_K
    top_v, top_i = lax.top_k(logits, TOP_K)
    gates = jax.nn.softmax(top_v, axis=-1)
    experts = jnp.arange(n_exp, dtype=jnp.int32)
    hit = top_i[:, :, None] == experts
    chosen = jnp.sum(hit, axis=1).astype(F32)
    counts = jnp.sum(chosen, axis=0).astype(jnp.int32)
    padded = (counts + tm - 1) // tm * tm
    pad_end = jnp.cumsum(padded)
    pad_start = pad_end - padded
    start = jnp.cumsum(counts) - counts
    ch = 256
    nch = -(-t // ch)
    cp = jnp.pad(chosen, ((0, nch * ch - t), (0, 0))).reshape(nch, ch, n_exp)
    tri = (jnp.arange(ch)[:, None] > jnp.arange(ch)[None, :]).astype(F32)
    within = jnp.einsum('ij,cje->cie', tri, cp)
    tot = jnp.sum(cp, axis=1)
    before = (within + (jnp.cumsum(tot, axis=0) - tot)[:, None, :]).reshape(nch * ch, n_exp)[:t]
    rank = jnp.sum(jnp.where(hit, before[:, None, :], 0.0), axis=-1).astype(jnp.int32)
    pos = jnp.sum(jnp.where(hit, pad_start, 0), axis=-1).astype(jnp.int32) + rank
    flat_e = top_i.reshape(-1).astype(jnp.int32)
    keys = jnp.sort(flat_e * a + jnp.arange(a, dtype=jnp.int32))
    order = keys - (keys // a) * a
    nblk = -(-a // tm) + n_exp
    blk_first = jnp.arange(nblk, dtype=jnp.int32) * tm
    blk_exp = jnp.minimum(jnp.sum(pad_end[None, :] <= blk_first[:, None], axis=1), n_exp - 1).astype(jnp.int32)
    n_used = (pad_end[-1] // tm).astype(jnp.int32)
    r0 = blk_first - pad_start[blk_exp]
    r = r0[:, None] + jnp.arange(tm, dtype=jnp.int32)[None, :]
    valid = (r < counts[blk_exp][:, None]) & (jnp.arange(nblk)[:, None] < n_used)
    src = jnp.clip(start[blk_exp][:, None] + r, 0, a - 1)
    slot_tok = jnp.where(valid, jnp.take(order, src, mode="clip") // TOP_K, t).reshape(-1).astype(jnp.int32)
    blk_exp = jnp.where(jnp.arange(nblk) < n_used, blk_exp, blk_exp[n_used - 1]).astype(jnp.int32)
    later = jnp.where((experts[None, :] > experts[:, None]) & (counts[None, :] > 0), experts[None, :], n_exp)
    nxt = jnp.min(later, axis=1)
    blk_next = jnp.where(nxt < n_exp, nxt, -1)[blk_exp].astype(jnp.int32)
    return gates, slot_tok, pos, blk_exp, blk_next, n_used.reshape(1)


def _pad_seq(t, n):
    return t if t.shape[1] == n else jnp.pad(t, ((0, 0), (0, n - t.shape[1]), (0, 0)))


def _mlstm_layer(u, gates_pre, w_in, norm_g, w_out, j, carry):
    b, s, d = u.shape
    c0, n0, m0 = carry
    h_, dk, dv = c0.shape[1:]
    L = ML_L if s % ML_L == 0 else -(-s // ML_L_MIN) * ML_L_MIN
    sp = -(-s // L) * L
    if sp != s:
        u = _pad_seq(u, sp)
        padg = jnp.concatenate([jnp.full((b, sp - s, h_), NEG, F32), jnp.full((b, sp - s, h_), -NEG, F32)], -1)
        gates_pre = jnp.concatenate([gates_pre, padg], 1)
    nqkv = 2 * h_ * dk + h_ * dv
    u2 = u.reshape(b * sp, d)
    proj = matmul(u2, w_in, j, 0, nqkv, BF16)
    ogate = matmul(u2, w_in, j, nqkv, h_ * dv, F32)
    hg, c_f, n_f, m_f = mlstm_core(proj, ogate, gates_pre, norm_g, c0, n0, m0, b, sp, L)
    y = matmul(hg, w_out, j, 0, d, F32).reshape(b, sp, d)
    return y[:, :s], (c_f, n_f, m_f)


def _moba_layer(u, w_qkv, w_out, j, heads, past):
    b, s, d = u.shape
    hd = d // heads
    x2 = u.reshape(b * s, d)
    q, k, v = (matmul(x2, w_qkv, j, c * d, d, F32).reshape(b, s, d) for c in range(3))
    p0 = 0 if past is None else past[2].shape[1] * past[0].shape[2]
    tabs = rope_tables(p0 + jnp.arange(s), hd)
    q, k = rope(q, tabs, hd), rope(k, tabs, hd)
    if past is None:
        o = moba_prompt(q.reshape(b * s, d), k.reshape(b * s, d), v.reshape(b * s, d), b, s, heads)
    else:
        cache_k, cache_v, page_table = past
        o = moba_sample(q, k, v, cache_k, cache_v, j, page_table, heads).reshape(b * s, d)
    y = matmul(o, w_out, j, 0, d, F32).reshape(b, s, d)
    return y, (k.reshape(b, s, heads, hd), v.reshape(b, s, heads, hd))


def kernel(x_prompt, x_sample, c_prompt, c_sample, state_mlstm_C, state_mlstm_n, state_mlstm_m, cache_moba_k, cache_moba_v, page_table, state_pool, ada_w, ada_b, ln_g, ln_b, mlstm_w_in, mlstm_b_gates, mlstm_norm_g, mlstm_w_out, moba_w_qkv, moba_w_out, pool_w, pool_scale, router_w, router_b, moe_w_gu, moe_b_gu, moe_w_down, moe_b_down):
    depth, d = ada_w.shape[0], ada_w.shape[1]
    bp, sp_, _ = x_prompt.shape
    bs, ss, _ = x_sample.shape
    n_exp = router_w.shape[2]
    ml_heads, ml_dk, ml_dv = state_mlstm_C.shape[2:]
    moba_heads = cache_moba_k.shape[3]
    alpha = (2.0 * depth) ** 0.25
    nproj = 2 * ml_heads * (ml_dk + ml_dv)

    nreq = bp + bs
    rpad = -(-nreq // 8) * 8
    c_all = jnp.pad(jnp.concatenate([c_prompt, c_sample], 0), ((0, rpad - nreq), (0, 0)))
    mods_all = ada_mod(c_all, ada_w, ada_b)
    groups = [dict(x=x_prompt, lo=0, b=bp), dict(x=x_sample, lo=bp, b=bs)]

    def mods_of(i, grp):
        return mods_all[i, grp['lo']:grp['lo'] + grp['b']].reshape(grp['b'], 1, 6 * d)

    def mixer_aux(i):
        if i % N_MIXERS == 0:
            jj = i // N_MIXERS
            return mlstm_w_in[:, :, nproj:][jj], mlstm_b_gates[jj]
        return None, None

    def u_dtype(i):
        return F32 if i % N_MIXERS == 2 else BF16

    aw, ab = mixer_aux(0)
    for grp in groups:
        outs = ln_mod(grp['x'], mods_of(0, grp), mod_c=(0, 1), aux_w=aw, aux_b=ab, u_dtype=u_dtype(0))
        grp['u'] = outs[0]
        grp['aux'] = outs[1] if aw is not None else None

    ml, mb, plst = ([], []), ([], []), ([], [])
    for i in range(depth):
        kind, j = i % N_MIXERS, i // N_MIXERS
        ys = []
        for gi, grp in enumerate(groups):
            u = grp['u']
            b, s, _ = u.shape
            if kind == 0:
                if gi == 0:
                    carry = (jnp.zeros((b, ml_heads, ml_dk, ml_dv), F32), jnp.zeros((b, ml_heads, ml_dk), F32),
                             jnp.full((b, ml_heads), ML_M_INIT, F32))
                else:
                    carry = (state_mlstm_C[j].astype(F32), state_mlstm_n[j].astype(F32), state_mlstm_m[j].astype(F32))
                y, st = _mlstm_layer(u, grp['aux'], mlstm_w_in, mlstm_norm_g[j], mlstm_w_out, j, carry)
                ml[gi].append(st)
            elif kind == 1:
                past = None if gi == 0 else (cache_moba_k, cache_moba_v, page_table)
                y, st = _moba_layer(u, moba_w_qkv, moba_w_out, j, moba_heads, past)
                mb[gi].append(st)
            else:
                wmax = max(POOL_WINDOWS)
                if gi == 0:
                    hist = jnp.zeros((b, wmax, d), F32)
                    ext = u
                else:
                    hist = jnp.pad(state_pool[j].astype(F32), ((0, 0), (1, 0), (0, 0)))
                    ext = jnp.concatenate([state_pool[j].astype(F32), u], 1)
                y = pool_mix(u, hist, pool_w, j, pool_scale[j])
                plst[gi].append(ext[:, ext.shape[1] - (wmax - 1):])
            ys.append(y)

        for grp, y in zip(groups, ys):
            grp['x'], grp['u'], grp['aux'] = ln_mod(
                grp['x'], mods_of(i, grp), y=y, gate_c=2, ln_g=ln_g[i, 0], ln_b=ln_b[i, 0], mod_c=(3, 4),
                aux_w=router_w[i], aux_b=router_b[i], u_dtype=BF16, alpha=alpha)

        toks = [g_['u'].reshape(-1, d) for g_ in groups]
        logits = jnp.concatenate([g_['aux'].reshape(-1, n_exp) for g_ in groups], 0)
        t_all = logits.shape[0]
        gates, slot_tok, pos, blk_exp, blk_next, n_used = moe_route(logits, n_exp)
        x_pad = jnp.concatenate(toks + [jnp.zeros((1, d), BF16)], 0)
        xs = jnp.take(x_pad, slot_tok, axis=0, mode="clip")
        ys_slots = moe_experts(xs, blk_exp, blk_next, n_used, moe_w_gu, moe_b_gu, moe_w_down, moe_b_down, i)

        last = i == depth - 1
        aw, ab = (None, None) if last else mixer_aux(i + 1)
        t0 = 0
        for grp in groups:
            b, s, _ = grp['x'].shape
            n = b * s
            pg = pos[t0:t0 + n]
            ysg = jnp.take(ys_slots, pg.T.reshape(-1), axis=0, mode="clip").reshape(TOP_K, b, s, d)
            gt = gates[t0:t0 + n].reshape(b, s, TOP_K)
            t0 += n
            outs = ln_mod(grp['x'], mods_of(i, grp) if last else jnp.concatenate([mods_of(i, grp), mods_of(i + 1, grp)], -1),
                          ys=ysg, gates=gt, gate_c=5, ln_g=ln_g[i, 1], ln_b=ln_b[i, 1],
                          mod_c=None if last else (6, 7), aux_w=aw, aux_b=ab,
                          u_dtype=BF16 if last else u_dtype(i + 1), alpha=alpha)
            grp['x'] = outs[0]
            if not last:
                grp['u'] = outs[1]
                grp['aux'] = outs[2] if aw is not None else None

    xdt = x_prompt.dtype
    sdt = state_mlstm_C.dtype
    stack = lambda sts, k, dt: jnp.stack([s_[k] for s_ in sts]).astype(dt)
    return (groups[0]['x'], groups[1]['x'],
            stack(ml[0], 0, xdt), stack(ml[0], 1, xdt), stack(ml[0], 2, xdt),
            stack(ml[1], 0, sdt), stack(ml[1], 1, sdt), stack(ml[1], 2, sdt),
            stack(mb[0], 0, xdt), stack(mb[0], 1, xdt), stack(mb[1], 0, xdt), stack(mb[1], 1, xdt),
            jnp.stack(plst[0]), jnp.stack(plst[1]))
```
